```python
import jax
import jax.numpy as jnp
from jax import lax
import numpy as np


D_MODEL = 2048
BATCH = 2
SEQ = 16384
DEPTH = 4

GRID_W = 64
CTX_LEN = 256

GLA_HEADS = 4
GLA_DK = 256
GLA_DV = 256
GLA_KEY_W = GLA_HEADS * GLA_DK
GLA_VAL_W = GLA_HEADS * GLA_DV
GLA_GATE_RANK = 16
GLA_TAU = 16.0
GLA_CHUNK = 64

CONV_W = D_MODEL // 2
CONV_K = 3

POOL_W = D_MODEL
POOL_WINDOWS = (2, 4, 8, 16)
POOL_GROUP = POOL_W // len(POOL_WINDOWS)

E_SIZES = (GLA_KEY_W, GLA_VAL_W, GLA_GATE_RANK, GLA_GATE_RANK, GLA_KEY_W, GLA_VAL_W,
           CONV_W, CONV_W, CONV_W, CONV_W)
E_IN = sum(E_SIZES)
E_STATE_COLS = GLA_KEY_W + GLA_VAL_W + 2 * GLA_GATE_RANK
E_MIX = GLA_VAL_W + CONV_W
O_IN = 2 * POOL_W
LN_EPS = 1e-5

kernel_name = 'hybrid_gla_shortconv_pool_deepnorm_prefix'


def layer_norm(x, g, b):
    xf = x.astype(jnp.float32)
    mu = jnp.mean(xf, axis=-1, keepdims=True)
    var = jnp.mean(jnp.square(xf - mu), axis=-1, keepdims=True)
    return ((xf - mu) * lax.rsqrt(var + LN_EPS)).astype(x.dtype) * g + b


def rms_norm(x, g):
    xf = x.astype(jnp.float32)
    return xf * lax.rsqrt(jnp.mean(jnp.square(xf), axis=-1, keepdims=True) + LN_EPS) * g.astype(jnp.float32)


def modulation(cond, w, b, n):
    m = jax.nn.silu(cond) @ w[:, :n * D_MODEL] + b[:n * D_MODEL]
    return jnp.split(m, n, axis=-1)


def split_cols(h, sizes):
    idx = [int(i) for i in np.cumsum(sizes)[:-1]]
    return jnp.split(h, idx, axis=-1)


def to_heads(t, d):
    bn, L, _ = t.shape
    return t.reshape(bn, L, -1, d).transpose(0, 2, 1, 3)


def from_heads(t):
    bn, H, L, d = t.shape
    return t.transpose(0, 2, 1, 3).reshape(bn, L, H * d)


def flip_seq(t):
    return jnp.flip(t, axis=2)


def gate_log(r, w, b):
    return jax.nn.log_sigmoid((r @ w + b).astype(jnp.float32)) / GLA_TAU


def gla_chunked(q, k, v, log_a, s0):
    bn, H, L, _ = q.shape
    dv = v.shape[-1]
    C = GLA_CHUNK
    nc = L // C

    def to_chunks(t):
        return jnp.moveaxis(t.astype(jnp.float32).reshape(bn, H, nc, C, t.shape[-1]), 2, 0)

    lower = np.tril(np.ones((C, C), dtype=bool))[:, :, None]

    def step(s, inp):
        qc, kc, vc, lac = inp
        g = jnp.cumsum(lac, axis=-2)
        diff = g[..., :, None, :] - g[..., None, :, :]
        decay = jnp.exp(jnp.where(lower, diff, -jnp.inf))
        scores = jnp.einsum('bhik,bhjk,bhijk->bhij', qc, kc, decay)
        o = (jnp.einsum('bhij,bhjv->bhiv', scores, vc)
             + jnp.einsum('bhik,bhkv->bhiv', qc * jnp.exp(g), s))
        g_last = g[..., -1:, :]
        s_new = (jnp.exp(g_last)[..., 0, :, None] * s
                 + jnp.einsum('bhjk,bhjv->bhkv', kc * jnp.exp(g_last - g), vc))
        return s_new, o

    s_fin, o = lax.scan(step, s0, (to_chunks(q), to_chunks(k), to_chunks(v), to_chunks(log_a)))
    o = jnp.moveaxis(o, 0, 2).reshape(bn, H, L, dv)
    return o, s_fin


def gla_final_state(k, v, log_a):
    g = jnp.cumsum(log_a, axis=-2)
    return jnp.einsum('bhtk,bhtv->bhkv', k * jnp.exp(g[..., -1:, :] - g), v)


def context_states(hs, w_gf, b_gf, w_gb, b_gb):
    k, v, rf, rb = split_cols(hs, E_SIZES[:4])
    k = to_heads(k, GLA_DK).astype(jnp.float32)
    v = to_heads(v, GLA_DV).astype(jnp.float32)
    la_f = to_heads(gate_log(rf, w_gf, b_gf), GLA_DK)
    la_b = to_heads(gate_log(rb, w_gb, b_gb), GLA_DK)
    s_f = gla_final_state(k, v, la_f)
    s_b = gla_final_state(flip_seq(k), flip_seq(v), flip_seq(la_b))
    return s_f, s_b


def conv3_centred(x, w):
    zero = jnp.zeros_like(x[..., :1, :])
    x_prev = jnp.concatenate([zero, x[..., :-1, :]], axis=-2)
    x_next = jnp.concatenate([x[..., 1:, :], zero], axis=-2)
    return w[0] * x_prev + w[1] * x + w[2] * x_next


def centred_window_mean(x, window, axis):
    L = x.shape[axis]
    xf = jnp.moveaxis(x, axis, 0).astype(jnp.float32)
    cs = jnp.concatenate([jnp.zeros_like(xf[:1]), jnp.cumsum(xf, axis=0)], axis=0)
    t = np.arange(L)
    lo = np.clip(t - window // 2, 0, L)
    hi = np.clip(t + window - window // 2, 0, L)
    cnt = (hi - lo).astype(np.float32).reshape((L,) + (1,) * (xf.ndim - 1))
    mean = (cs[hi] - cs[lo]) / cnt
    return jnp.moveaxis(mean, 0, axis).astype(x.dtype)


def even_mixer(h, s0_f, s0_b, w_gf, b_gf, w_gb, b_gb, gla_norm_w, conv_w, w_out, grid):
    bn, L, _ = h.shape
    k, v, rf, rb, q, gate_b, a_b, a_c, a_x, gate_a = split_cols(h, E_SIZES)
    qh = to_heads(q * (GLA_DK ** -0.5), GLA_DK)
    kh = to_heads(k, GLA_DK)
    vh = to_heads(v, GLA_DV)
    la_f = to_heads(gate_log(rf, w_gf, b_gf), GLA_DK)
    la_b = to_heads(gate_log(rb, w_gb, b_gb), GLA_DK)
    o_f, s_f = gla_chunked(qh, kh, vh, la_f, s0_f)
    o_b, s_b = gla_chunked(flip_seq(qh), flip_seq(kh), flip_seq(vh), flip_seq(la_b), s0_b)
    o = rms_norm(o_f + flip_seq(o_b), gla_norm_w)
    y_b = from_heads(o).astype(h.dtype) * jax.nn.silu(gate_b)
    u = (a_c * a_x).reshape(bn, grid[0], grid[1], CONV_W)
    u = conv3_centred(u, conv_w).reshape(bn, L, CONV_W)
    y_a = a_b * u * jax.nn.silu(gate_a)
    return jnp.concatenate([y_b, y_a], axis=-1) @ w_out, s_f, s_b


def odd_mixer(h, w_pool, pool_scale, w_out, grid):
    bn, L, _ = h.shape
    xp, gate = jnp.split(h, 2, axis=-1)
    xg = xp.reshape(bn, grid[0], grid[1], POOL_W)
    groups = jnp.split(xg, len(POOL_WINDOWS), axis=-1)
    z = jnp.stack([centred_window_mean(g, w, 1) - g for g, w in zip(groups, POOL_WINDOWS)], axis=-2)
    z = jnp.einsum('brcgi,gio->brcgo', z, w_pool).reshape(bn, L, POOL_W) * pool_scale
    return (z * jax.nn.silu(gate)) @ w_out


def setup_inputs(seed: int = 0) -> dict:
    key = jax.random.key(seed)
    ks = jax.random.split(key, 20)
    ne = (DEPTH + 1) // 2
    no = DEPTH // 2
    beta = (8.0 * DEPTH) ** -0.25

    def nrm(k, shape, s):
        return jax.random.normal(k, shape, jnp.float32) * s

    return {
        'x': nrm(ks[0], (BATCH, SEQ, D_MODEL), 1.0),
        'c': nrm(ks[1], (BATCH, D_MODEL), 1.0),
        'ctx': nrm(ks[2], (BATCH, CTX_LEN, D_MODEL), 1.0),
        'c_ctx': nrm(ks[3], (D_MODEL,), 1.0),
        'w_ada': nrm(ks[4], (DEPTH, D_MODEL, 3 * D_MODEL), 0.5 * D_MODEL ** -0.5),
        'b_ada': nrm(ks[5], (DEPTH, 3 * D_MODEL), 0.02),
        'ln_g': 1.0 + nrm(ks[6], (DEPTH, D_MODEL), 0.02),
        'ln_b': nrm(ks[7], (DEPTH, D_MODEL), 0.02),
        'w_in_e': nrm(ks[8], (ne, D_MODEL, E_IN), D_MODEL ** -0.5),
        'w_gate_f': nrm(ks[9], (ne, GLA_GATE_RANK, GLA_KEY_W), GLA_GATE_RANK ** -0.5),
        'b_gate_f': nrm(ks[10], (ne, GLA_KEY_W), 0.1),
        'w_gate_b': nrm(ks[11], (ne, GLA_GATE_RANK, GLA_KEY_W), GLA_GATE_RANK ** -0.5),
        'b_gate_b': nrm(ks[12], (ne, GLA_KEY_W), 0.1),
        'gla_norm_w': 1.0 + nrm(ks[13], (ne, GLA_DV), 0.02),
        'conv_w': nrm(ks[14], (ne, CONV_K, CONV_W), CONV_K ** -0.5),
        'w_out_e': nrm(ks[15], (ne, E_MIX, D_MODEL), beta * E_MIX ** -0.5),
        'w_in_o': nrm(ks[16], (no, D_MODEL, O_IN), D_MODEL ** -0.5),
        'w_pool': nrm(ks[17], (no, len(POOL_WINDOWS), POOL_GROUP, POOL_GROUP), POOL_GROUP ** -0.5),
        'pool_scale': 1.0 + nrm(ks[18], (no, POOL_W), 0.02),
        'w_out_o': nrm(ks[19], (no, POOL_W, D_MODEL), beta * POOL_W ** -0.5),
    }


def reference(x, c, ctx, c_ctx, w_ada, b_ada, ln_g, ln_b, w_in_e, w_gate_f, b_gate_f,
              w_gate_b, b_gate_b, gla_norm_w, conv_w, w_out_e, w_in_o, w_pool, pool_scale, w_out_o):
    alpha = (2.0 * DEPTH) ** 0.25
    bn, L, _ = x.shape
    rows = L // GRID_W
    lc = ctx.shape[1]
    c_lat = c[:, None, :]
    c_c = c_ctx[None, None, :]
    s_zero = jnp.zeros((bn, GLA_HEADS, GLA_DK, GLA_DV), jnp.float32)
    ctx_s = ctx
    for i in range(DEPTH):
        j = i // 2
        ctx_needed = any(l % 2 == 0 for l in range(i + 1, DEPTH))
        sh, sc, gt = modulation(c_lat, w_ada[i], b_ada[i], 3)
        u = x * (1.0 + sc) + sh
        if i % 2 == 0:
            if ctx_needed:
                sh_c, sc_c, gt_c = modulation(c_c, w_ada[i], b_ada[i], 3)
                uc = ctx_s * (1.0 + sc_c) + sh_c
                yc, s_f, s_b = even_mixer(uc @ w_in_e[j], s_zero, s_zero, w_gate_f[j], b_gate_f[j],
                                          w_gate_b[j], b_gate_b[j], gla_norm_w[j], conv_w[j],
                                          w_out_e[j], (1, lc))
                ctx_s = layer_norm(alpha * ctx_s + gt_c * yc, ln_g[i], ln_b[i])
            else:
                sh_c, sc_c = modulation(c_c, w_ada[i], b_ada[i], 2)
                uc = ctx_s * (1.0 + sc_c) + sh_c
                s_f, s_b = context_states(uc @ w_in_e[j][:, :E_STATE_COLS], w_gate_f[j], b_gate_f[j],
                                          w_gate_b[j], b_gate_b[j])
            y, _, _ = even_mixer(u @ w_in_e[j], s_f, s_b, w_gate_f[j], b_gate_f[j], w_gate_b[j],
                                 b_gate_b[j], gla_norm_w[j], conv_w[j], w_out_e[j], (rows, GRID_W))
        else:
            if ctx_needed:
                sh_c, sc_c, gt_c = modulation(c_c, w_ada[i], b_ada[i], 3)
                uc = ctx_s * (1.0 + sc_c) + sh_c
                yc = odd_mixer(uc @ w_in_o[j], w_pool[j], pool_scale[j], w_out_o[j], (lc, 1))
                ctx_s = layer_norm(alpha * ctx_s + gt_c * yc, ln_g[i], ln_b[i])
            y = odd_mixer(u @ w_in_o[j], w_pool[j], pool_scale[j], w_out_o[j], (rows, GRID_W))
        x = layer_norm(alpha * x + gt * y, ln_g[i], ln_b[i])
    return x
```

```python
import functools

import jax
import jax.numpy as jnp
from jax import lax
from jax.experimental import pallas as pl
from jax.experimental.pallas import tpu as pltpu

D_MODEL = 2048
DEPTH = 4
GRID_W = 64

GLA_HEADS = 4
GLA_DK = 256
GLA_DV = 256
GLA_KEY_W = GLA_HEADS * GLA_DK
GLA_VAL_W = GLA_HEADS * GLA_DV
GLA_GATE_RANK = 16
GLA_TAU = 16.0
CONV_W = D_MODEL // 2
POOL_W = D_MODEL
POOL_WINDOWS = (2, 4, 8, 16)
POOL_GROUP = POOL_W // len(POOL_WINDOWS)
POOL_HALO_ROWS = max(POOL_WINDOWS) // 2
LN_EPS = 1e-5
ALPHA = (2.0 * DEPTH) ** 0.25

LANES = 128
COND_ROWS = 8
VMEM_LIMIT = 60000 * 1024
GLA_CHUNK = 128
ROW_TILE = 256
POOL_TILE = 512

F32 = jnp.float32
BF16 = jnp.bfloat16
HIGHEST = lax.Precision.HIGHEST


def _silu(t):
    return t / (1.0 + jnp.exp(-t))


def _resident(shape):
    return pl.BlockSpec(shape, lambda *_: (0,) * len(shape), pipeline_mode=pl.Buffered(1))


def _params(*sem):
    return pltpu.CompilerParams(dimension_semantics=sem, vmem_limit_bytes=VMEM_LIMIT)


def _modulation_kernel(c_ref, w_ref, b_ref, o_ref):
    s = _silu(c_ref[...])
    o_ref[...] = jnp.dot(s, w_ref[...], precision=HIGHEST, preferred_element_type=F32) + b_ref[...]


def _modulation(cond, w_ada, b_ada):
    depth, d, n = w_ada.shape
    tn = 768
    return pl.pallas_call(
        _modulation_kernel,
        grid=(depth, n // tn),
        in_specs=[
            pl.BlockSpec((COND_ROWS, d), lambda i, j: (0, 0)),
            pl.BlockSpec((None, d, tn), lambda i, j: (i, 0, j)),
            pl.BlockSpec((None, 1, tn), lambda i, j: (i, 0, j)),
        ],
        out_specs=pl.BlockSpec((None, COND_ROWS, tn), lambda i, j: (i, 0, j)),
        out_shape=jax.ShapeDtypeStruct((depth, COND_ROWS, n), F32),
        compiler_params=_params("parallel", "parallel"),
        name="modulation",
    )(cond, w_ada, b_ada.reshape(depth, 1, n))


def _even_in_kernel(x_ref, sh_ref, sc_ref, w_ref, wr_ref, cw_ref,
                    kvq_ref, gb_ref, ya_ref, r_ref, *, row_w):
    tm = x_ref.shape[0]
    u = (x_ref[...] * (1.0 + sc_ref[...]) + sh_ref[...]).astype(BF16)

    def proj(col):
        return jnp.dot(u, w_ref[:, col:col + GLA_KEY_W], preferred_element_type=F32)

    kvq_ref[:, 0:1024] = proj(0).astype(BF16)
    kvq_ref[:, 1024:2048] = proj(1024).astype(BF16)
    kvq_ref[:, 2048:3072] = (proj(2048) * GLA_DK ** -0.5).astype(BF16)
    gb_ref[...] = _silu(proj(3072)).astype(BF16)
    r_ref[...] = jnp.dot(u, wr_ref[...], preferred_element_type=F32)

    p = proj(5120) * proj(6144)
    t = lax.broadcasted_iota(jnp.int32, (tm, 1), 0) % row_w
    prev = jnp.where(t == 0, 0.0, pltpu.roll(p, 1, axis=0))
    nxt = jnp.where(t == row_w - 1, 0.0, pltpu.roll(p, tm - 1, axis=0))
    conv = cw_ref[0:1, :] * prev + cw_ref[1:2, :] * p + cw_ref[2:3, :] * nxt
    ya_ref[...] = (proj(4096) * conv * _silu(proj(7168))).astype(BF16)


def _even_in(x, sh, sc, w_main, w_r, conv_w, row_w):
    bn, L, d = x.shape
    tm = min(ROW_TILE, L)
    assert L % tm == 0 and tm % row_w == 0
    tile = lambda n: pl.BlockSpec((None, tm, n), lambda b, i: (b, i, 0))
    vec = pl.BlockSpec((None, 1, d), lambda b, i: (b, 0, 0))
    return pl.pallas_call(
        functools.partial(_even_in_kernel, row_w=row_w),
        grid=(bn, L // tm),
        in_specs=[tile(d), vec, vec, _resident(w_main.shape), _resident(w_r.shape),
                  _resident(conv_w.shape)],
        out_specs=[tile(3 * GLA_KEY_W), tile(GLA_VAL_W), tile(CONV_W), tile(LANES)],
        out_shape=[jax.ShapeDtypeStruct((bn, L, 3 * GLA_KEY_W), BF16),
                   jax.ShapeDtypeStruct((bn, L, GLA_VAL_W), BF16),
                   jax.ShapeDtypeStruct((bn, L, CONV_W), BF16),
                   jax.ShapeDtypeStruct((bn, L, LANES), F32)],
        compiler_params=_params("parallel", "parallel"),
        name="even_in",
    )(x, sh, sc, w_main, w_r, conv_w)


def _gla_kernel(k_ref, v_ref, q_ref, r_ref, wg_ref, bg_ref, s0_ref, o_ref, sfin_ref, st_scr):
    d = pl.program_id(0)
    c = pl.program_id(3)
    chunk = k_ref.shape[0]

    @pl.when(c == 0)
    def _():
        st_scr[...] = s0_ref[...]

    z = jnp.dot(r_ref[...], wg_ref[...], precision=HIGHEST, preferred_element_type=F32) + bg_ref[...]
    la = (jnp.minimum(z, 0.0) - jnp.log1p(jnp.exp(-jnp.abs(z)))) * (1.0 / GLA_TAU)

    row = lax.broadcasted_iota(jnp.int32, (chunk, chunk), 0)
    col = lax.broadcasted_iota(jnp.int32, (chunk, chunk), 1)
    fwd = d == 0
    tri = jnp.where(fwd, row - col, col - row) >= 0
    g = jnp.dot(tri.astype(F32), la, precision=HIGHEST, preferred_element_type=F32)
    g_tot = jnp.where(fwd, g[chunk - 1:chunk, :], g[0:1, :])

    k = k_ref[...].astype(F32)
    v = v_ref[...]
    qg = (q_ref[...].astype(F32) * jnp.exp(g)).astype(BF16)
    kg = (k * jnp.exp(-g)).astype(BF16)
    kd = (k * jnp.exp(g_tot - g)).astype(BF16)

    a = lax.dot_general(qg, kg, (((1,), (1,)), ((), ())), preferred_element_type=F32)
    a = jnp.where(tri, a, 0.0).astype(BF16)
    st = st_scr[...]
    o = jnp.dot(a, v, preferred_element_type=F32)
    o += lax.dot_general(qg, st.astype(BF16), (((1,), (1,)), ((), ())), preferred_element_type=F32)
    o_ref[...] = o

    st_new = st * jnp.exp(g_tot) + lax.dot_general(
        v, kd, (((0,), (0,)), ((), ())), preferred_element_type=F32)
    st_scr[...] = st_new

    @pl.when(c == pl.num_programs(3) - 1)
    def _():
        sfin_ref[...] = st_new


def _gla(kvq, r, wg, bg, s0):
    bn, L, _ = kvq.shape
    chunk = min(GLA_CHUNK, L)
    nc = L // chunk
    assert L % chunk == 0

    def cidx(d, c):
        return c + d * (nc - 1 - 2 * c)

    def head_block(off):
        return pl.BlockSpec((None, chunk, GLA_DK), lambda d, b, h, c: (b, cidx(d, c), off + h))

    state = pl.BlockSpec((None, None, None, GLA_DV, GLA_DK), lambda d, b, h, c: (d, b, h, 0, 0))
    return pl.pallas_call(
        _gla_kernel,
        grid=(2, bn, GLA_HEADS, nc),
        in_specs=[head_block(0), head_block(GLA_HEADS), head_block(2 * GLA_HEADS),
                  pl.BlockSpec((None, chunk, LANES), lambda d, b, h, c: (b, cidx(d, c), 0)),
                  pl.BlockSpec((None, LANES, GLA_DK), lambda d, b, h, c: (d, 0, h)),
                  pl.BlockSpec((None, 1, GLA_DK), lambda d, b, h, c: (d, 0, h)),
                  state],
        out_specs=[pl.BlockSpec((None, None, chunk, GLA_DV), lambda d, b, h, c: (d, b, cidx(d, c), h)),
                   state],
        out_shape=[jax.ShapeDtypeStruct((2, bn, L, GLA_VAL_W), F32),
                   jax.ShapeDtypeStruct((2, bn, GLA_HEADS, GLA_DV, GLA_DK), F32)],
        scratch_shapes=[pltpu.VMEM((GLA_DV, GLA_DK), F32)],
        compiler_params=_params("parallel", "parallel", "parallel", "arbitrary"),
        name="gla",
    )(kvq, kvq, kvq, r, wg, bg, s0)


def _residual_ln(x, y, gt, g, b):
    xn = ALPHA * x + gt * y
    mu = jnp.mean(xn, axis=-1, keepdims=True)
    xc = xn - mu
    var = jnp.mean(xc * xc, axis=-1, keepdims=True)
    return xc * lax.rsqrt(var + LN_EPS) * g + b


def _even_out_kernel(of_ref, ob_ref, gb_ref, ya_ref, nw_ref, w_ref, x_ref, gt_ref, lg_ref, lb_ref,
                     out_ref):
    o = of_ref[...] + ob_ref[...]
    heads = []
    for h in range(GLA_HEADS):
        oh = o[:, h * GLA_DV:(h + 1) * GLA_DV]
        ms = jnp.mean(oh * oh, axis=-1, keepdims=True)
        heads.append(oh * lax.rsqrt(ms + LN_EPS) * nw_ref[...])
    yb = (jnp.concatenate(heads, axis=-1) * gb_ref[...].astype(F32)).astype(BF16)
    y = jnp.dot(yb, w_ref[0:GLA_VAL_W, :], preferred_element_type=F32)
    y += jnp.dot(ya_ref[...], w_ref[GLA_VAL_W:, :], preferred_element_type=F32)
    out_ref[...] = _residual_ln(x_ref[...], y, gt_ref[...], lg_ref[...], lb_ref[...])


def _even_out(o, gb, ya, norm_w, w_out, x, gt, ln_g, ln_b):
    bn, L, d = x.shape
    tm = min(ROW_TILE, L)
    tile = lambda n: pl.BlockSpec((None, tm, n), lambda b, i: (b, i, 0))
    odir = lambda k: pl.BlockSpec((None, None, tm, GLA_VAL_W), lambda b, i: (k, b, i, 0))
    return pl.pallas_call(
        _even_out_kernel,
        grid=(bn, L // tm),
        in_specs=[odir(0), odir(1), tile(GLA_VAL_W), tile(CONV_W), _resident(norm_w.shape),
                  _resident(w_out.shape), tile(d),
                  pl.BlockSpec((None, 1, d), lambda b, i: (b, 0, 0)),
                  _resident(ln_g.shape), _resident(ln_b.shape)],
        out_specs=tile(d),
        out_shape=jax.ShapeDtypeStruct((bn, L, d), F32),
        compiler_params=_params("parallel", "parallel"),
        name="even_out",
    )(o, o, gb, ya, norm_w, w_out, x, gt, ln_g, ln_b)


def _out_ln_kernel(m_ref, w_ref, x_ref, gt_ref, lg_ref, lb_ref, out_ref):
    y = jnp.dot(m_ref[...], w_ref[...], preferred_element_type=F32)
    out_ref[...] = _residual_ln(x_ref[...], y, gt_ref[...], lg_ref[...], lb_ref[...])


def _out_ln(m, w_out, x, gt, ln_g, ln_b):
    bn, L, d = x.shape
    tm = min(ROW_TILE, L)
    tile = lambda n: pl.BlockSpec((None, tm, n), lambda b, i: (b, i, 0))
    return pl.pallas_call(
        _out_ln_kernel,
        grid=(bn, L // tm),
        in_specs=[tile(m.shape[-1]), _resident(w_out.shape), tile(d),
                  pl.BlockSpec((None, 1, d), lambda b, i: (b, 0, 0)),
                  _resident(ln_g.shape), _resident(ln_b.shape)],
        out_specs=tile(d),
        out_shape=jax.ShapeDtypeStruct((bn, L, d), F32),
        compiler_params=_params("parallel", "parallel"),
        name="out_ln",
    )(m, w_out, x, gt, ln_g, ln_b)


def _odd_mix_kernel(xp_ref, xc_ref, xn_ref, sh_ref, sc_ref, win_ref, wpool_ref, ps_ref, m_ref,
                    *, row_stride, n_rows):
    i = pl.program_id(1)
    tm = xc_ref.shape[0]
    halo = POOL_HALO_ROWS * row_stride
    xc = xc_ref[...]
    before = jnp.where(i > 0, xp_ref[tm - halo:, :], 0.0)
    after = jnp.where(i < pl.num_programs(1) - 1, xn_ref[:halo, :], 0.0)
    xe = jnp.concatenate([before, xc, after], axis=0)

    scale1 = 1.0 + sc_ref[...]
    u = (xc * scale1 + sh_ref[...]).astype(BF16)
    grid_row = (i * tm + lax.broadcasted_iota(jnp.int32, (tm, 1), 0)) // row_stride

    s = xe
    for g, w in enumerate(POOL_WINDOWS):
        half = w // 2
        s = s[:s.shape[0] - half * row_stride, :] + s[half * row_stride:, :]
        start = halo - half * row_stride
        cnt = jnp.minimum(grid_row + half, n_rows) - jnp.maximum(grid_row - half, 0)
        mean = s[start:start + tm, :] / cnt.astype(F32)
        diff = ((mean - xc) * scale1).astype(BF16)
        cols = slice(g * POOL_GROUP, (g + 1) * POOL_GROUP)
        z = jnp.dot(diff, win_ref[:, cols], preferred_element_type=F32)
        z = jnp.dot(z.astype(BF16), wpool_ref[g], preferred_element_type=F32)
        gate = jnp.dot(u, win_ref[:, POOL_W + g * POOL_GROUP:POOL_W + (g + 1) * POOL_GROUP],
                       preferred_element_type=F32)
        m_ref[:, cols] = (z * ps_ref[:, cols] * _silu(gate)).astype(BF16)


def _odd_mix(x, sh, sc, w_in, w_pool, pool_scale, row_stride):
    bn, L, d = x.shape
    n_rows = L // row_stride
    tm = min(POOL_TILE, L)
    nt = L // tm
    assert L % tm == 0 and tm % row_stride == 0 and POOL_HALO_ROWS * row_stride <= tm
    vec = pl.BlockSpec((None, 1, d), lambda b, i: (b, 0, 0))
    return pl.pallas_call(
        functools.partial(_odd_mix_kernel, row_stride=row_stride, n_rows=n_rows),
        grid=(bn, nt),
        in_specs=[pl.BlockSpec((None, tm, d), lambda b, i: (b, jnp.maximum(i - 1, 0), 0)),
                  pl.BlockSpec((None, tm, d), lambda b, i: (b, i, 0)),
                  pl.BlockSpec((None, tm, d), lambda b, i: (b, jnp.minimum(i + 1, nt - 1), 0)),
                  vec, vec, _resident(w_in.shape), _resident(w_pool.shape),
                  _resident(pool_scale.shape)],
        out_specs=pl.BlockSpec((None, tm, POOL_W), lambda b, i: (b, i, 0)),
        out_shape=jax.ShapeDtypeStruct((bn, L, POOL_W), BF16),
        compiler_params=_params("parallel", "parallel"),
        name="odd_mix",
    )(x, x, x, sh, sc, w_in, w_pool, pool_scale)


def _even_weights(w_in, w_gf, b_gf, w_gb, b_gb):
    k, v, rf, rb, q, gate_b, a_b, a_c, a_x, gate_a = jnp.split(
        w_in, [1024, 2048, 2064, 2080, 3104, 4128, 5152, 6176, 7200], axis=1)
    w_main = jnp.concatenate([k, v, q, gate_b, a_b, a_c, a_x, gate_a], axis=1).astype(BF16)
    pad = jnp.zeros((D_MODEL, LANES - 2 * GLA_GATE_RANK), F32)
    w_r = jnp.concatenate([rf, rb, pad], axis=1).astype(BF16)
    zr = jnp.zeros((GLA_GATE_RANK, GLA_KEY_W), F32)
    tail = jnp.zeros((LANES - 2 * GLA_GATE_RANK, GLA_KEY_W), F32)
    wg = jnp.stack([jnp.concatenate([w_gf, zr, tail], axis=0),
                    jnp.concatenate([zr, w_gb, tail], axis=0)])
    bg = jnp.stack([b_gf, b_gb])[:, None, :]
    return w_main, w_r, wg, bg


def kernel(x, c, ctx, c_ctx, w_ada, b_ada, ln_g, ln_b, w_in_e, w_gate_f, b_gate_f, w_gate_b,
           b_gate_b, gla_norm_w, conv_w, w_out_e, w_in_o, w_pool, pool_scale, w_out_o):
    bn, L, d = x.shape
    lc = ctx.shape[1]
    cond = jnp.concatenate([c, c_ctx[None, :], jnp.zeros((COND_ROWS - bn - 1, d), F32)], axis=0)
    mod = _modulation(cond, w_ada, b_ada)
    s_zero = jnp.zeros((2, bn, GLA_HEADS, GLA_DV, GLA_DK), F32)
    ctx_s = ctx
    for i in range(DEPTH):
        j = i // 2
        ctx_needed = any(l % 2 == 0 for l in range(i + 1, DEPTH))
        sh, sc, gt = (mod[i, :bn, n * d:(n + 1) * d][:, None, :] for n in range(3))
        sh_c, sc_c, gt_c = (jnp.broadcast_to(mod[i, bn, n * d:(n + 1) * d], (bn, 1, d))
                            for n in range(3))
        lg, lb = ln_g[i][None, :], ln_b[i][None, :]
        if i % 2 == 0:
            w_main, w_r, wg, bg = _even_weights(w_in_e[j], w_gate_f[j], b_gate_f[j],
                                                w_gate_b[j], b_gate_b[j])
            w_out = w_out_e[j].astype(BF16)
            norm_w = gla_norm_w[j][None, :]
            kvq_c, gb_c, ya_c, r_c = _even_in(ctx_s, sh_c, sc_c, w_main, w_r, conv_w[j], lc)
            o_c, s_ctx = _gla(kvq_c, r_c, wg, bg, s_zero)
            if ctx_needed:
                ctx_s = _even_out(o_c, gb_c, ya_c, norm_w, w_out, ctx_s, gt_c, lg, lb)
            kvq, gb, ya, r = _even_in(x, sh, sc, w_main, w_r, conv_w[j], GRID_W)
            o, _ = _gla(kvq, r, wg, bg, s_ctx)
            x = _even_out(o, gb, ya, norm_w, w_out, x, gt, lg, lb)
        else:
            w_in = w_in_o[j].astype(BF16)
            w_out = w_out_o[j].astype(BF16)
            wp = w_pool[j].astype(BF16)
            ps = pool_scale[j][None, :]
            if ctx_needed:
                m_c = _odd_mix(ctx_s, sh_c, sc_c, w_in, wp, ps, 1)
                ctx_s = _out_ln(m_c, w_out, ctx_s, gt_c, lg, lb)
            m = _odd_mix(x, sh, sc, w_in, wp, ps, GRID_W)
            x = _out_ln(m, w_out, x, gt, lg, lb)
    return x
```

```python
import functools

import jax
import jax.numpy as jnp
from jax import lax
from jax.experimental import pallas as pl
from jax.experimental.pallas import tpu as pltpu

D_MODEL = 2048
DEPTH = 4
GRID_W = 64

GLA_HEADS = 4
GLA_DK = 256
GLA_DV = 256
GLA_KEY_W = GLA_HEADS * GLA_DK
GLA_VAL_W = GLA_HEADS * GLA_DV
GLA_GATE_RANK = 16
GLA_TAU = 16.0
CONV_W = D_MODEL // 2
POOL_W = D_MODEL
POOL_WINDOWS = (2, 4, 8, 16)
POOL_GROUP = POOL_W // len(POOL_WINDOWS)
POOL_HALO_ROWS = max(POOL_WINDOWS) // 2
LN_EPS = 1e-5
ALPHA = (2.0 * DEPTH) ** 0.25

LANES = 128
COND_ROWS = 8
VMEM_LIMIT = 60000 * 1024
GLA_CHUNK = 256
ROW_TILE = 256
POOL_TILE = 512
RANK_COLS = 2 * GLA_GATE_RANK

F32 = jnp.float32
BF16 = jnp.bfloat16
HIGHEST = lax.Precision.HIGHEST

_NT = (((1,), (1,)), ((), ()))
_TN = (((0,), (0,)), ((), ()))


def _silu(t):
    return t / (1.0 + jnp.exp(-t))


def _resident(shape):
    return pl.BlockSpec(shape, lambda *_: (0,) * len(shape), pipeline_mode=pl.Buffered(1))


def _params(*sem):
    return pltpu.CompilerParams(dimension_semantics=sem, vmem_limit_bytes=VMEM_LIMIT)


def _modulation_kernel(c_ref, w_ref, b_ref, o_ref):
    s = _silu(c_ref[...])
    o_ref[...] = jnp.dot(s, w_ref[...], precision=HIGHEST, preferred_element_type=F32) + b_ref[...]


def _modulation(cond, w_ada, b_ada):
    depth, d, n = w_ada.shape
    tn = 768
    return pl.pallas_call(
        _modulation_kernel,
        grid=(depth, n // tn),
        in_specs=[
            pl.BlockSpec((COND_ROWS, d), lambda i, j: (0, 0)),
            pl.BlockSpec((None, d, tn), lambda i, j: (i, 0, j)),
            pl.BlockSpec((None, 1, tn), lambda i, j: (i, 0, j)),
        ],
        out_specs=pl.BlockSpec((None, COND_ROWS, tn), lambda i, j: (i, 0, j)),
        out_shape=jax.ShapeDtypeStruct((depth, COND_ROWS, n), F32),
        compiler_params=_params("parallel", "parallel"),
        name="modulation",
    )(cond, w_ada, b_ada.reshape(depth, 1, n))


def _even_in_kernel(x_ref, sh_ref, sc_ref, w_ref, wr_ref, cw_ref,
                    kvq_ref, gb_ref, ya_ref, r_ref, *, row_w):
    tm = x_ref.shape[0]
    u = (x_ref[...] * (1.0 + sc_ref[...]) + sh_ref[...]).astype(BF16)

    def proj(col):
        return jnp.dot(u, w_ref[:, col:col + GLA_KEY_W], preferred_element_type=F32)

    kvq_ref[:, 0:1024] = proj(0).astype(BF16)
    kvq_ref[:, 1024:2048] = proj(1024).astype(BF16)
    kvq_ref[:, 2048:3072] = (proj(2048) * GLA_DK ** -0.5).astype(BF16)
    gb_ref[...] = _silu(proj(3072)).astype(BF16)

    r = jnp.dot(u, wr_ref[...], preferred_element_type=F32)
    r_hi = r.astype(BF16)
    r_lo = (r - r_hi.astype(F32)).astype(BF16)
    lane = lax.broadcasted_iota(jnp.int32, r.shape, 1)
    r_ref[...] = jnp.where((lane >= RANK_COLS) & (lane < 2 * RANK_COLS), r_lo, r_hi)

    p = proj(5120) * proj(6144)
    t = lax.broadcasted_iota(jnp.int32, (tm, 1), 0) % row_w
    prev = jnp.where(t == 0, 0.0, pltpu.roll(p, 1, axis=0))
    nxt = jnp.where(t == row_w - 1, 0.0, pltpu.roll(p, tm - 1, axis=0))
    conv = cw_ref[0:1, :] * prev + cw_ref[1:2, :] * p + cw_ref[2:3, :] * nxt
    ya_ref[...] = (proj(4096) * conv * _silu(proj(7168))).astype(BF16)


def _even_in(x, sh, sc, w_main, w_r, conv_w, row_w):
    bn, L, d = x.shape
    tm = min(ROW_TILE, L)
    assert L % tm == 0 and tm % row_w == 0
    tile = lambda n: pl.BlockSpec((None, tm, n), lambda b, i: (b, i, 0))
    vec = pl.BlockSpec((None, 1, d), lambda b, i: (b, 0, 0))
    return pl.pallas_call(
        functools.partial(_even_in_kernel, row_w=row_w),
        grid=(bn, L // tm),
        in_specs=[tile(d), vec, vec, _resident(w_main.shape), _resident(w_r.shape),
                  _resident(conv_w.shape)],
        out_specs=[tile(3 * GLA_KEY_W), tile(GLA_VAL_W), tile(CONV_W), tile(LANES)],
        out_shape=[jax.ShapeDtypeStruct((bn, L, 3 * GLA_KEY_W), BF16),
                   jax.ShapeDtypeStruct((bn, L, GLA_VAL_W), BF16),
                   jax.ShapeDtypeStruct((bn, L, CONV_W), BF16),
                   jax.ShapeDtypeStruct((bn, L, LANES), BF16)],
        compiler_params=_params("parallel", "parallel"),
        name="even_in",
    )(x, sh, sc, w_main, w_r, conv_w)


def _gla_kernel(k_ref, v_ref, q_ref, r_ref, wg_ref, bg_ref, s0_ref, o_ref, sfin_ref, st_scr):
    d = pl.program_id(0)
    c = pl.program_id(2)
    chunk = k_ref.shape[0]

    @pl.when(c == 0)
    def _():
        st_scr[...] = s0_ref[...]

    row = lax.broadcasted_iota(jnp.int32, (chunk, chunk), 0)
    col = lax.broadcasted_iota(jnp.int32, (chunk, chunk), 1)
    fwd = d == 0
    tri = jnp.where(fwd, row - col, col - row) >= 0
    tri_b = jnp.where(tri, 1.0, 0.0).astype(BF16)
    tri2 = jnp.concatenate([tri_b, tri_b], axis=1)

    z = jnp.dot(r_ref[...], wg_ref[...], preferred_element_type=F32) + bg_ref[...]
    la = (jnp.minimum(z, 0.0) - jnp.log(1.0 + jnp.exp(-jnp.abs(z)))) * (1.0 / GLA_TAU)
    la_hi = la.astype(BF16)
    la_lo = (la - la_hi.astype(F32)).astype(BF16)
    g = jnp.dot(tri2, jnp.concatenate([la_hi, la_lo], axis=0), preferred_element_type=F32)
    p = g[chunk // 2:chunk // 2 + 1, :]
    g_tot = jnp.where(fwd, g[chunk - 1:chunk, :], g[0:1, :])
    qg_all = (q_ref[...].astype(F32) * jnp.exp(g - p)).astype(BF16)
    kp = k_ref[...].astype(F32) * jnp.exp(p - g)
    kg_all = kp.astype(BF16)
    kd_all = (kp * jnp.exp(g_tot - p)).astype(BF16)
    e_p = jnp.exp(p)
    e_tot = jnp.exp(g_tot)

    for h in range(GLA_HEADS):
        hs = slice(h * GLA_DK, (h + 1) * GLA_DK)
        qg = qg_all[:, hs]
        v = v_ref[:, hs]
        a = lax.dot_general(qg, kg_all[:, hs], _NT, preferred_element_type=F32)
        a = jnp.where(tri, a, 0.0).astype(BF16)
        st = st_scr[h]
        o = jnp.dot(a, v, preferred_element_type=F32)
        o += lax.dot_general(qg, (st * e_p[:, hs]).astype(BF16), _NT, preferred_element_type=F32)
        o_ref[:, hs] = o
        st_scr[h] = st * e_tot[:, hs] + lax.dot_general(v, kd_all[:, hs], _TN,
                                                        preferred_element_type=F32)

    @pl.when(c == pl.num_programs(2) - 1)
    def _():
        sfin_ref[...] = st_scr[...]


def _gla(kvq, r3, wg3, bg, s0):
    bn, L, _ = kvq.shape
    chunk = min(GLA_CHUNK, L)
    nc = L // chunk
    assert L % chunk == 0

    def cidx(d, c):
        return c + d * (nc - 1 - 2 * c)

    def part(n):
        return pl.BlockSpec((None, chunk, GLA_KEY_W), lambda d, b, c: (b, cidx(d, c), n))

    state = pl.BlockSpec((None, None, GLA_HEADS, GLA_DV, GLA_DK), lambda d, b, c: (d, b, 0, 0, 0))
    return pl.pallas_call(
        _gla_kernel,
        grid=(2, bn, nc),
        in_specs=[part(0), part(1), part(2),
                  pl.BlockSpec((None, chunk, LANES), lambda d, b, c: (b, cidx(d, c), 0)),
                  pl.BlockSpec((None, LANES, GLA_KEY_W), lambda d, b, c: (d, 0, 0)),
                  pl.BlockSpec((None, 1, GLA_KEY_W), lambda d, b, c: (d, 0, 0)),
                  state],
        out_specs=[pl.BlockSpec((None, None, chunk, GLA_VAL_W), lambda d, b, c: (d, b, cidx(d, c), 0)),
                   state],
        out_shape=[jax.ShapeDtypeStruct((2, bn, L, GLA_VAL_W), F32),
                   jax.ShapeDtypeStruct((2, bn, GLA_HEADS, GLA_DV, GLA_DK), F32)],
        scratch_shapes=[pltpu.VMEM((GLA_HEADS, GLA_DV, GLA_DK), F32)],
        compiler_params=_params("parallel", "parallel", "arbitrary"),
        name="gla",
    )(kvq, kvq, kvq, r3, wg3, bg, s0)


def _residual_ln(x, y, gt, g, b):
    xn = ALPHA * x + gt * y
    mu = jnp.mean(xn, axis=-1, keepdims=True)
    xc = xn - mu
    var = jnp.mean(xc * xc, axis=-1, keepdims=True)
    return xc * lax.rsqrt(var + LN_EPS) * g + b


def _even_out_kernel(of_ref, ob_ref, gb_ref, ya_ref, nw_ref, w_ref, x_ref, gt_ref, lg_ref, lb_ref,
                     out_ref):
    o = of_ref[...] + ob_ref[...]
    heads = []
    for h in range(GLA_HEADS):
        oh = o[:, h * GLA_DV:(h + 1) * GLA_DV]
        ms = jnp.mean(oh * oh, axis=-1, keepdims=True)
        heads.append(oh * lax.rsqrt(ms + LN_EPS) * nw_ref[...])
    yb = (jnp.concatenate(heads, axis=-1) * gb_ref[...].astype(F32)).astype(BF16)
    y = jnp.dot(yb, w_ref[0:GLA_VAL_W, :], preferred_element_type=F32)
    y += jnp.dot(ya_ref[...], w_ref[GLA_VAL_W:, :], preferred_element_type=F32)
    out_ref[...] = _residual_ln(x_ref[...], y, gt_ref[...], lg_ref[...], lb_ref[...])


def _even_out(o, gb, ya, norm_w, w_out, x, gt, ln_g, ln_b):
    bn, L, d = x.shape
    tm = min(ROW_TILE, L)
    tile = lambda n: pl.BlockSpec((None, tm, n), lambda b, i: (b, i, 0))
    odir = lambda k: pl.BlockSpec((None, None, tm, GLA_VAL_W), lambda b, i: (k, b, i, 0))
    return pl.pallas_call(
        _even_out_kernel,
        grid=(bn, L // tm),
        in_specs=[odir(0), odir(1), tile(GLA_VAL_W), tile(CONV_W), _resident(norm_w.shape),
                  _resident(w_out.shape), tile(d),
                  pl.BlockSpec((None, 1, d), lambda b, i: (b, 0, 0)),
                  _resident(ln_g.shape), _resident(ln_b.shape)],
        out_specs=tile(d),
        out_shape=jax.ShapeDtypeStruct((bn, L, d), F32),
        compiler_params=_params("parallel", "parallel"),
        name="even_out",
    )(o, o, gb, ya, norm_w, w_out, x, gt, ln_g, ln_b)


def _out_ln_kernel(m_ref, w_ref, x_ref, gt_ref, lg_ref, lb_ref, out_ref):
    y = jnp.dot(m_ref[...], w_ref[...], preferred_element_type=F32)
    out_ref[...] = _residual_ln(x_ref[...], y, gt_ref[...], lg_ref[...], lb_ref[...])


def _out_ln(m, w_out, x, gt, ln_g, ln_b):
    bn, L, d = x.shape
    tm = min(ROW_TILE, L)
    tile = lambda n: pl.BlockSpec((None, tm, n), lambda b, i: (b, i, 0))
    return pl.pallas_call(
        _out_ln_kernel,
        grid=(bn, L // tm),
        in_specs=[tile(m.shape[-1]), _resident(w_out.shape), tile(d),
                  pl.BlockSpec((None, 1, d), lambda b, i: (b, 0, 0)),
                  _resident(ln_g.shape), _resident(ln_b.shape)],
        out_specs=tile(d),
        out_shape=jax.ShapeDtypeStruct((bn, L, d), F32),
        compiler_params=_params("parallel", "parallel"),
        name="out_ln",
    )(m, w_out, x, gt, ln_g, ln_b)


def _odd_mix_kernel(xp_ref, xc_ref, xn_ref, sh_ref, sc_ref, win_ref, wpool_ref, ps_ref, m_ref,
                    *, row_stride, n_rows):
    i = pl.program_id(1)
    tm = xc_ref.shape[0]
    halo = POOL_HALO_ROWS * row_stride
    xc = xc_ref[...]
    before = jnp.where(i > 0, xp_ref[tm - halo:, :], 0.0)
    after = jnp.where(i < pl.num_programs(1) - 1, xn_ref[:halo, :], 0.0)
    xe = jnp.concatenate([before, xc, after], axis=0)

    scale1 = 1.0 + sc_ref[...]
    u = (xc * scale1 + sh_ref[...]).astype(BF16)
    grid_row = (i * tm + lax.broadcasted_iota(jnp.int32, (tm, 1), 0)) // row_stride

    s = xe
    for g, w in enumerate(POOL_WINDOWS):
        half = w // 2
        s = s[:s.shape[0] - half * row_stride, :] + s[half * row_stride:, :]
        start = halo - half * row_stride
        cnt = jnp.minimum(grid_row + half, n_rows) - jnp.maximum(grid_row - half, 0)
        mean = s[start:start + tm, :] / cnt.astype(F32)
        diff = ((mean - xc) * scale1).astype(BF16)
        cols = slice(g * POOL_GROUP, (g + 1) * POOL_GROUP)
        z = jnp.dot(diff, win_ref[:, cols], preferred_element_type=F32)
        z = jnp.dot(z.astype(BF16), wpool_ref[g], preferred_element_type=F32)
        gate = jnp.dot(u, win_ref[:, POOL_W + g * POOL_GROUP:POOL_W + (g + 1) * POOL_GROUP],
                       preferred_element_type=F32)
        m_ref[:, cols] = (z * ps_ref[:, cols] * _silu(gate)).astype(BF16)


def _odd_mix(x, sh, sc, w_in, w_pool, pool_scale, row_stride):
    bn, L, d = x.shape
    n_rows = L // row_stride
    tm = min(POOL_TILE, L)
    nt = L // tm
    assert L % tm == 0 and tm % row_stride == 0 and POOL_HALO_ROWS * row_stride <= tm
    vec = pl.BlockSpec((None, 1, d), lambda b, i: (b, 0, 0))
    return pl.pallas_call(
        functools.partial(_odd_mix_kernel, row_stride=row_stride, n_rows=n_rows),
        grid=(bn, nt),
        in_specs=[pl.BlockSpec((None, tm, d), lambda b, i: (b, jnp.maximum(i - 1, 0), 0)),
                  pl.BlockSpec((None, tm, d), lambda b, i: (b, i, 0)),
                  pl.BlockSpec((None, tm, d), lambda b, i: (b, jnp.minimum(i + 1, nt - 1), 0)),
                  vec, vec, _resident(w_in.shape), _resident(w_pool.shape),
                  _resident(pool_scale.shape)],
        out_specs=pl.BlockSpec((None, tm, POOL_W), lambda b, i: (b, i, 0)),
        out_shape=jax.ShapeDtypeStruct((bn, L, POOL_W), BF16),
        compiler_params=_params("parallel", "parallel"),
        name="odd_mix",
    )(x, x, x, sh, sc, w_in, w_pool, pool_scale)


def _hi_lo(w):
    hi = w.astype(BF16)
    return hi, (w - hi.astype(F32)).astype(BF16)


def _even_weights(w_in, w_gf, b_gf, w_gb, b_gb):
    k, v, rf, rb, q, gate_b, a_b, a_c, a_x, gate_a = jnp.split(
        w_in, [1024, 2048, 2064, 2080, 3104, 4128, 5152, 6176, 7200], axis=1)
    w_main = jnp.concatenate([k, v, q, gate_b, a_b, a_c, a_x, gate_a], axis=1).astype(BF16)
    pad = jnp.zeros((D_MODEL, LANES - 3 * RANK_COLS), F32)
    w_r = jnp.concatenate([rf, rb] * 3 + [pad], axis=1).astype(BF16)
    zr = jnp.zeros((GLA_GATE_RANK, GLA_KEY_W), BF16)
    tail = jnp.zeros((LANES - 3 * RANK_COLS, GLA_KEY_W), BF16)
    f_hi, f_lo = _hi_lo(w_gf)
    b_hi, b_lo = _hi_lo(w_gb)
    wg3 = jnp.stack([jnp.concatenate([f_hi, zr, f_hi, zr, f_lo, zr, tail], axis=0),
                     jnp.concatenate([zr, b_hi, zr, b_hi, zr, b_lo, tail], axis=0)])
    bg = jnp.stack([b_gf, b_gb])[:, None, :]
    return w_main, w_r, wg3, bg


def kernel(x, c, ctx, c_ctx, w_ada, b_ada, ln_g, ln_b, w_in_e, w_gate_f, b_gate_f, w_gate_b,
           b_gate_b, gla_norm_w, conv_w, w_out_e, w_in_o, w_pool, pool_scale, w_out_o):
    bn, L, d = x.shape
    lc = ctx.shape[1]
    cond = jnp.concatenate([c, c_ctx[None, :], jnp.zeros((COND_ROWS - bn - 1, d), F32)], axis=0)
    mod = _modulation(cond, w_ada, b_ada)
    s_zero = jnp.zeros((2, bn, GLA_HEADS, GLA_DV, GLA_DK), F32)
    ctx_s = ctx
    for i in range(DEPTH):
        j = i // 2
        ctx_needed = any(l % 2 == 0 for l in range(i + 1, DEPTH))
        sh, sc, gt = (mod[i, :bn, n * d:(n + 1) * d][:, None, :] for n in range(3))
        sh_c, sc_c, gt_c = (jnp.broadcast_to(mod[i, bn, n * d:(n + 1) * d], (bn, 1, d))
                            for n in range(3))
        lg, lb = ln_g[i][None, :], ln_b[i][None, :]
        if i % 2 == 0:
            w_main, w_r, wg3, bg = _even_weights(w_in_e[j], w_gate_f[j], b_gate_f[j],
                                                 w_gate_b[j], b_gate_b[j])
            w_out = w_out_e[j].astype(BF16)
            norm_w = gla_norm_w[j][None, :]
            kvq_c, gb_c, ya_c, r_c = _even_in(ctx_s, sh_c, sc_c, w_main, w_r, conv_w[j], lc)
            o_c, s_ctx = _gla(kvq_c, r_c, wg3, bg, s_zero)
            if ctx_needed:
                ctx_s = _even_out(o_c, gb_c, ya_c, norm_w, w_out, ctx_s, gt_c, lg, lb)
            kvq, gb, ya, r = _even_in(x, sh, sc, w_main, w_r, conv_w[j], GRID_W)
            o, _ = _gla(kvq, r, wg3, bg, s_ctx)
            x = _even_out(o, gb, ya, norm_w, w_out, x, gt, lg, lb)
        else:
            w_in = w_in_o[j].astype(BF16)
            w_out = w_out_o[j].astype(BF16)
            wp = w_pool[j].astype(BF16)
            ps = pool_scale[j][None, :]
            if ctx_needed:
                m_c = _odd_mix(ctx_s, sh_c, sc_c, w_in, wp, ps, 1)
                ctx_s = _out_ln(m_c, w_out, ctx_s, gt_c, lg, lb)
            m = _odd_mix(x, sh, sc, w_in, wp, ps, GRID_W)
            x = _out_ln(m, w_out, x, gt, lg, lb)
    return x
```

```python
import functools

import jax
import jax.numpy as jnp
from jax import lax
from jax.experimental import pallas as pl
from jax.experimental.pallas import tpu as pltpu

D_MODEL = 2048
DEPTH = 4
GRID_W = 64

GLA_HEADS = 4
GLA_DK = 256
GLA_DV = 256
GLA_KEY_W = GLA_HEADS * GLA_DK
GLA_VAL_W = GLA_HEADS * GLA_DV
GLA_GATE_RANK = 16
GLA_TAU = 16.0
CONV_W = D_MODEL // 2
POOL_W = D_MODEL
POOL_WINDOWS = (2, 4, 8, 16)
POOL_GROUP = POOL_W // len(POOL_WINDOWS)
POOL_HALO_ROWS = max(POOL_WINDOWS) // 2
LN_EPS = 1e-5
ALPHA = (2.0 * DEPTH) ** 0.25

LANES = 128
COND_ROWS = 8
VMEM_LIMIT = 60000 * 1024
GLA_CHUNK = 256
ROW_TILE = 256
SUB_TILE = 256
EVEN_OUT_TILE = 512
OUT_LN_TILE = 1024
POOL_TILE = 512
RANK_COLS = 2 * GLA_GATE_RANK

F32 = jnp.float32
BF16 = jnp.bfloat16
HIGHEST = lax.Precision.HIGHEST

_NT = (((1,), (1,)), ((), ()))
_TN = (((0,), (0,)), ((), ()))


def _silu(t):
    return t / (1.0 + jnp.exp(-t))


def _resident(shape):
    return pl.BlockSpec(shape, lambda *_: (0,) * len(shape), pipeline_mode=pl.Buffered(1))


def _layer_resident(stacked_shape, layer):
    nd = len(stacked_shape)
    return pl.BlockSpec((None,) + tuple(stacked_shape[1:]), lambda *_: (layer,) + (0,) * (nd - 1),
                        pipeline_mode=pl.Buffered(1))


def _params(*sem):
    return pltpu.CompilerParams(dimension_semantics=sem, vmem_limit_bytes=VMEM_LIMIT)


def _modulation_kernel(c_ref, w_ref, b_ref, o_ref):
    s = _silu(c_ref[...])
    o_ref[...] = jnp.dot(s, w_ref[...], precision=HIGHEST, preferred_element_type=F32) + b_ref[...]


def _modulation(cond, w_ada, b_ada):
    depth, d, n = w_ada.shape
    tn = 768
    return pl.pallas_call(
        _modulation_kernel,
        grid=(depth, n // tn),
        in_specs=[
            pl.BlockSpec((COND_ROWS, d), lambda i, j: (0, 0)),
            pl.BlockSpec((None, d, tn), lambda i, j: (i, 0, j)),
            pl.BlockSpec((None, 1, tn), lambda i, j: (i, 0, j)),
        ],
        out_specs=pl.BlockSpec((None, COND_ROWS, tn), lambda i, j: (i, 0, j)),
        out_shape=jax.ShapeDtypeStruct((depth, COND_ROWS, n), F32),
        compiler_params=_params("parallel", "parallel"),
        name="modulation",
    )(cond, w_ada, b_ada.reshape(depth, 1, n))


def _even_in_kernel(x_ref, sh_ref, sc_ref, w_ref, wr_ref, cw_ref,
                    kvq_ref, gb_ref, ya_ref, r_ref, *, row_w):
    tm = x_ref.shape[0]
    u = (x_ref[...] * (1.0 + sc_ref[...]) + sh_ref[...]).astype(BF16)

    def proj(col):
        return jnp.dot(u, w_ref[:, col:col + GLA_KEY_W], preferred_element_type=F32)

    kvq_ref[:, 0:1024] = proj(0).astype(BF16)
    kvq_ref[:, 1024:2048] = proj(1024).astype(BF16)
    kvq_ref[:, 2048:3072] = (proj(2048) * GLA_DK ** -0.5).astype(BF16)
    gb_ref[...] = _silu(proj(3072)).astype(BF16)

    r = jnp.dot(u, wr_ref[...], preferred_element_type=F32)
    r_hi = r.astype(BF16)
    r_lo = (r - r_hi.astype(F32)).astype(BF16)
    lane = lax.broadcasted_iota(jnp.int32, r.shape, 1)
    r_ref[...] = jnp.where((lane >= RANK_COLS) & (lane < 2 * RANK_COLS), r_lo, r_hi)

    p = proj(5120) * proj(6144)
    t = lax.broadcasted_iota(jnp.int32, (tm, 1), 0) % row_w
    prev = jnp.where(t == 0, 0.0, pltpu.roll(p, 1, axis=0))
    nxt = jnp.where(t == row_w - 1, 0.0, pltpu.roll(p, tm - 1, axis=0))
    conv = cw_ref[0:1, :] * prev + cw_ref[1:2, :] * p + cw_ref[2:3, :] * nxt
    ya_ref[...] = (proj(4096) * conv * _silu(proj(7168))).astype(BF16)


def _even_in(x, sh, sc, w_main, w_r, conv_w, row_w):
    bn, L, d = x.shape
    tm = min(ROW_TILE, L)
    assert L % tm == 0 and tm % row_w == 0
    tile = lambda n: pl.BlockSpec((None, tm, n), lambda b, i: (b, i, 0))
    vec = pl.BlockSpec((None, 1, d), lambda b, i: (b, 0, 0))
    return pl.pallas_call(
        functools.partial(_even_in_kernel, row_w=row_w),
        grid=(bn, L // tm),
        in_specs=[tile(d), vec, vec, _resident(w_main.shape), _resident(w_r.shape),
                  _resident(conv_w.shape)],
        out_specs=[tile(3 * GLA_KEY_W), tile(GLA_VAL_W), tile(CONV_W), tile(LANES)],
        out_shape=[jax.ShapeDtypeStruct((bn, L, 3 * GLA_KEY_W), BF16),
                   jax.ShapeDtypeStruct((bn, L, GLA_VAL_W), BF16),
                   jax.ShapeDtypeStruct((bn, L, CONV_W), BF16),
                   jax.ShapeDtypeStruct((bn, L, LANES), BF16)],
        compiler_params=_params("parallel", "parallel"),
        name="even_in",
    )(x, sh, sc, w_main, w_r, conv_w)


def _gla_kernel(k_ref, v_ref, q_ref, r_ref, wg_ref, bg_ref, s0_ref, o_ref, sfin_ref, st_scr):
    d = pl.program_id(0)
    c = pl.program_id(2)
    chunk = k_ref.shape[0]

    @pl.when(c == 0)
    def _():
        st_scr[...] = s0_ref[...]

    row = lax.broadcasted_iota(jnp.int32, (chunk, chunk), 0)
    col = lax.broadcasted_iota(jnp.int32, (chunk, chunk), 1)
    fwd = d == 0
    tri = jnp.where(fwd, row - col, col - row) >= 0
    tri_b = jnp.where(tri, 1.0, 0.0).astype(BF16)
    tri2 = jnp.concatenate([tri_b, tri_b], axis=1)

    z = jnp.dot(r_ref[...], wg_ref[...], preferred_element_type=F32) + bg_ref[...]
    la = (jnp.minimum(z, 0.0) - jnp.log(1.0 + jnp.exp(-jnp.abs(z)))) * (1.0 / GLA_TAU)
    la_hi = la.astype(BF16)
    la_lo = (la - la_hi.astype(F32)).astype(BF16)
    g = jnp.dot(tri2, jnp.concatenate([la_hi, la_lo], axis=0), preferred_element_type=F32)
    p = g[chunk // 2:chunk // 2 + 1, :]
    g_tot = jnp.where(fwd, g[chunk - 1:chunk, :], g[0:1, :])
    qg_all = (q_ref[...].astype(F32) * jnp.exp(g - p)).astype(BF16)
    kp = k_ref[...].astype(F32) * jnp.exp(p - g)
    kg_all = kp.astype(BF16)
    kd_all = (kp * jnp.exp(g_tot - p)).astype(BF16)
    e_p = jnp.exp(p)
    e_tot = jnp.exp(g_tot)

    for h in range(GLA_HEADS):
        hs = slice(h * GLA_DK, (h + 1) * GLA_DK)
        qg = qg_all[:, hs]
        v = v_ref[:, hs]
        a = lax.dot_general(qg, kg_all[:, hs], _NT, preferred_element_type=F32)
        a = jnp.where(tri, a, 0.0).astype(BF16)
        st = st_scr[h]
        o = jnp.dot(a, v, preferred_element_type=F32)
        o += lax.dot_general(qg, (st * e_p[:, hs]).astype(BF16), _NT, preferred_element_type=F32)
        o_ref[:, hs] = o
        st_scr[h] = st * e_tot[:, hs] + lax.dot_general(v, kd_all[:, hs], _TN,
                                                        preferred_element_type=F32)

    @pl.when(c == pl.num_programs(2) - 1)
    def _():
        sfin_ref[...] = st_scr[...]


def _gla(kvq, r3, wg3, bg, s0):
    bn, L, _ = kvq.shape
    chunk = min(GLA_CHUNK, L)
    nc = L // chunk
    assert L % chunk == 0

    def cidx(d, c):
        return c + d * (nc - 1 - 2 * c)

    def part(n):
        return pl.BlockSpec((None, chunk, GLA_KEY_W), lambda d, b, c: (b, cidx(d, c), n))

    state = pl.BlockSpec((None, None, GLA_HEADS, GLA_DV, GLA_DK), lambda d, b, c: (d, b, 0, 0, 0))
    return pl.pallas_call(
        _gla_kernel,
        grid=(2, bn, nc),
        in_specs=[part(0), part(1), part(2),
                  pl.BlockSpec((None, chunk, LANES), lambda d, b, c: (b, cidx(d, c), 0)),
                  pl.BlockSpec((None, LANES, GLA_KEY_W), lambda d, b, c: (d, 0, 0)),
                  pl.BlockSpec((None, 1, GLA_KEY_W), lambda d, b, c: (d, 0, 0)),
                  state],
        out_specs=[pl.BlockSpec((None, None, chunk, GLA_VAL_W), lambda d, b, c: (d, b, cidx(d, c), 0)),
                   state],
        out_shape=[jax.ShapeDtypeStruct((2, bn, L, GLA_VAL_W), F32),
                   jax.ShapeDtypeStruct((2, bn, GLA_HEADS, GLA_DV, GLA_DK), F32)],
        scratch_shapes=[pltpu.VMEM((GLA_HEADS, GLA_DV, GLA_DK), F32)],
        compiler_params=_params("parallel", "parallel", "arbitrary"),
        name="gla",
    )(kvq, kvq, kvq, r3, wg3, bg, s0)


def _residual_ln(x, y, gt, g, b):
    xn = ALPHA * x + gt * y
    mu = jnp.mean(xn, axis=-1, keepdims=True)
    xc = xn - mu
    var = jnp.mean(xc * xc, axis=-1, keepdims=True)
    return xc * lax.rsqrt(var + LN_EPS) * g + b


def _gla_branch(of_ref, ob_ref, gb_ref, nw_ref, rows):
    o = of_ref[rows, :] + ob_ref[rows, :]
    heads = []
    for h in range(GLA_HEADS):
        oh = o[:, h * GLA_DV:(h + 1) * GLA_DV]
        ms = jnp.mean(oh * oh, axis=-1, keepdims=True)
        heads.append(oh * lax.rsqrt(ms + LN_EPS) * nw_ref[...])
    return (jnp.concatenate(heads, axis=-1) * gb_ref[rows, :].astype(F32)).astype(BF16)


def _even_out_kernel(of_ref, ob_ref, gb_ref, ya_ref, nw_ref, w_ref, x_ref, gt_ref, lg_ref, lb_ref,
                     out_ref):
    n_sub = x_ref.shape[0] // SUB_TILE
    sub = lambda s: pl.ds(s * SUB_TILE, SUB_TILE)
    yb = _gla_branch(of_ref, ob_ref, gb_ref, nw_ref, sub(0))
    y = None
    for s in range(n_sub + 1):
        y_prev = y
        if s < n_sub:
            y = jnp.dot(yb, w_ref[0:GLA_VAL_W, :], preferred_element_type=F32)
            y += jnp.dot(ya_ref[sub(s), :], w_ref[GLA_VAL_W:, :], preferred_element_type=F32)
        if s + 1 < n_sub:
            yb = _gla_branch(of_ref, ob_ref, gb_ref, nw_ref, sub(s + 1))
        if s > 0:
            out_ref[sub(s - 1), :] = _residual_ln(x_ref[sub(s - 1), :], y_prev, gt_ref[...],
                                                  lg_ref[...], lb_ref[...])


def _even_out(o, gb, ya, norm_w, w_out, layer, x, gt, ln_g, ln_b):
    bn, L, d = x.shape
    tm = min(EVEN_OUT_TILE, L)
    assert L % tm == 0 and tm % SUB_TILE == 0
    tile = lambda n: pl.BlockSpec((None, tm, n), lambda b, i: (b, i, 0))
    odir = lambda k: pl.BlockSpec((None, None, tm, GLA_VAL_W), lambda b, i: (k, b, i, 0))
    return pl.pallas_call(
        _even_out_kernel,
        grid=(bn, L // tm),
        in_specs=[odir(0), odir(1), tile(GLA_VAL_W), tile(CONV_W), _resident(norm_w.shape),
                  _layer_resident(w_out.shape, layer), tile(d),
                  pl.BlockSpec((None, 1, d), lambda b, i: (b, 0, 0)),
                  _resident(ln_g.shape), _resident(ln_b.shape)],
        out_specs=tile(d),
        out_shape=jax.ShapeDtypeStruct((bn, L, d), F32),
        compiler_params=_params("parallel", "parallel"),
        name="even_out",
    )(o, o, gb, ya, norm_w, w_out, x, gt, ln_g, ln_b)


def _out_ln_kernel(m_ref, w_ref, x_ref, gt_ref, lg_ref, lb_ref, out_ref):
    n_sub = x_ref.shape[0] // SUB_TILE
    sub = lambda s: pl.ds(s * SUB_TILE, SUB_TILE)
    y = None
    for s in range(n_sub + 1):
        y_prev = y
        if s < n_sub:
            y = jnp.dot(m_ref[sub(s), :], w_ref[...], preferred_element_type=F32)
        if s > 0:
            out_ref[sub(s - 1), :] = _residual_ln(x_ref[sub(s - 1), :], y_prev, gt_ref[...],
                                                  lg_ref[...], lb_ref[...])


def _out_ln(m, w_out, layer, x, gt, ln_g, ln_b):
    bn, L, d = x.shape
    tm = min(OUT_LN_TILE, L)
    assert L % tm == 0 and tm % SUB_TILE == 0
    tile = lambda n: pl.BlockSpec((None, tm, n), lambda b, i: (b, i, 0))
    return pl.pallas_call(
        _out_ln_kernel,
        grid=(bn, L // tm),
        in_specs=[tile(m.shape[-1]), _layer_resident(w_out.shape, layer), tile(d),
                  pl.BlockSpec((None, 1, d), lambda b, i: (b, 0, 0)),
                  _resident(ln_g.shape), _resident(ln_b.shape)],
        out_specs=tile(d),
        out_shape=jax.ShapeDtypeStruct((bn, L, d), F32),
        compiler_params=_params("parallel", "parallel"),
        name="out_ln",
    )(m, w_out, x, gt, ln_g, ln_b)


def _odd_mix_kernel(xp_ref, xc_ref, xn_ref, sh_ref, sc_ref, win_ref, wpool_ref, ps_ref, m_ref,
                    *, row_stride, n_rows):
    i = pl.program_id(1)
    tm = xc_ref.shape[0]
    halo = POOL_HALO_ROWS * row_stride
    xc = xc_ref[...]
    scale1 = 1.0 + sc_ref[...]
    u = (xc * scale1 + sh_ref[...]).astype(BF16)
    sgate = _silu(jnp.dot(u, win_ref[:, POOL_W:], preferred_element_type=F32)).astype(BF16)

    before = jnp.where(i > 0, xp_ref[tm - halo:, :], 0.0)
    after = jnp.where(i < pl.num_programs(1) - 1, xn_ref[:halo, :], 0.0)
    xe = jnp.concatenate([before, xc, after], axis=0)
    grid_row = (i * tm + lax.broadcasted_iota(jnp.int32, (tm, 1), 0)) // row_stride

    def window_diff(s, w):
        half = w // 2
        s = s[:s.shape[0] - half * row_stride, :] + s[half * row_stride:, :]
        start = halo - half * row_stride
        cnt = jnp.minimum(grid_row + half, n_rows) - jnp.maximum(grid_row - half, 0)
        mean = s[start:start + tm, :] / cnt.astype(F32)
        return s, ((mean - xc) * scale1).astype(BF16)

    s, diff = window_diff(xe, POOL_WINDOWS[0])
    for g in range(len(POOL_WINDOWS)):
        cur = diff
        if g + 1 < len(POOL_WINDOWS):
            s, diff = window_diff(s, POOL_WINDOWS[g + 1])
        cols = slice(g * POOL_GROUP, (g + 1) * POOL_GROUP)
        z = jnp.dot(cur, win_ref[:, cols], preferred_element_type=F32)
        z = jnp.dot(z.astype(BF16), wpool_ref[g], preferred_element_type=F32)
        m_ref[:, cols] = (z * ps_ref[:, cols] * sgate[:, cols].astype(F32)).astype(BF16)


def _odd_mix(x, sh, sc, w_in, w_pool, layer, pool_scale, row_stride):
    bn, L, d = x.shape
    n_rows = L // row_stride
    tm = min(POOL_TILE, L)
    nt = L // tm
    assert L % tm == 0 and tm % row_stride == 0 and POOL_HALO_ROWS * row_stride <= tm
    vec = pl.BlockSpec((None, 1, d), lambda b, i: (b, 0, 0))
    return pl.pallas_call(
        functools.partial(_odd_mix_kernel, row_stride=row_stride, n_rows=n_rows),
        grid=(bn, nt),
        in_specs=[pl.BlockSpec((None, tm, d), lambda b, i: (b, jnp.maximum(i - 1, 0), 0)),
                  pl.BlockSpec((None, tm, d), lambda b, i: (b, i, 0)),
                  pl.BlockSpec((None, tm, d), lambda b, i: (b, jnp.minimum(i + 1, nt - 1), 0)),
                  vec, vec, _layer_resident(w_in.shape, layer), _layer_resident(w_pool.shape, layer),
                  _resident(pool_scale.shape)],
        out_specs=pl.BlockSpec((None, tm, POOL_W), lambda b, i: (b, i, 0)),
        out_shape=jax.ShapeDtypeStruct((bn, L, POOL_W), BF16),
        compiler_params=_params("parallel", "parallel"),
        name="odd_mix",
    )(x, x, x, sh, sc, w_in, w_pool, pool_scale)


def _hi_lo(w):
    hi = w.astype(BF16)
    return hi, (w - hi.astype(F32)).astype(BF16)


def _even_weights(w_in, w_gf, b_gf, w_gb, b_gb):
    k, v, rf, rb, q, gate_b, a_b, a_c, a_x, gate_a = jnp.split(
        w_in, [1024, 2048, 2064, 2080, 3104, 4128, 5152, 6176, 7200], axis=1)
    w_main = jnp.concatenate([k, v, q, gate_b, a_b, a_c, a_x, gate_a], axis=1)
    pad = jnp.zeros((D_MODEL, LANES - 3 * RANK_COLS), BF16)
    w_r = jnp.concatenate([rf, rb] * 3 + [pad], axis=1)
    zr = jnp.zeros((GLA_GATE_RANK, GLA_KEY_W), BF16)
    tail = jnp.zeros((LANES - 3 * RANK_COLS, GLA_KEY_W), BF16)
    f_hi, f_lo = _hi_lo(w_gf)
    b_hi, b_lo = _hi_lo(w_gb)
    wg3 = jnp.stack([jnp.concatenate([f_hi, zr, f_hi, zr, f_lo, zr, tail], axis=0),
                     jnp.concatenate([zr, b_hi, zr, b_hi, zr, b_lo, tail], axis=0)])
    bg = jnp.stack([b_gf, b_gb])[:, None, :]
    return w_main, w_r, wg3, bg


def kernel(x, c, ctx, c_ctx, w_ada, b_ada, ln_g, ln_b, w_in_e, w_gate_f, b_gate_f, w_gate_b,
           b_gate_b, gla_norm_w, conv_w, w_out_e, w_in_o, w_pool, pool_scale, w_out_o):
    bn, L, d = x.shape
    lc = ctx.shape[1]
    cond = jnp.concatenate([c, c_ctx[None, :], jnp.zeros((COND_ROWS - bn - 1, d), F32)], axis=0)
    mod = _modulation(cond, w_ada, b_ada)
    s_zero = jnp.zeros((2, bn, GLA_HEADS, GLA_DV, GLA_DK), F32)
    w_in_e, w_out_e, w_in_o, w_out_o, w_pool = (
        w.astype(BF16) for w in (w_in_e, w_out_e, w_in_o, w_out_o, w_pool))
    ctx_s = ctx
    for i in range(DEPTH):
        j = i // 2
        ctx_needed = any(l % 2 == 0 for l in range(i + 1, DEPTH))
        sh, sc, gt = (mod[i, :bn, n * d:(n + 1) * d][:, None, :] for n in range(3))
        sh_c, sc_c, gt_c = (jnp.broadcast_to(mod[i, bn, n * d:(n + 1) * d], (bn, 1, d))
                            for n in range(3))
        lg, lb = ln_g[i][None, :], ln_b[i][None, :]
        if i % 2 == 0:
            w_main, w_r, wg3, bg = _even_weights(w_in_e[j], w_gate_f[j], b_gate_f[j],
                                                 w_gate_b[j], b_gate_b[j])
            norm_w = gla_norm_w[j][None, :]
            kvq_c, gb_c, ya_c, r_c = _even_in(ctx_s, sh_c, sc_c, w_main, w_r, conv_w[j], lc)
            o_c, s_ctx = _gla(kvq_c, r_c, wg3, bg, s_zero)
            if ctx_needed:
                ctx_s = _even_out(o_c, gb_c, ya_c, norm_w, w_out_e, j, ctx_s, gt_c, lg, lb)
            kvq, gb, ya, r = _even_in(x, sh, sc, w_main, w_r, conv_w[j], GRID_W)
            o, _ = _gla(kvq, r, wg3, bg, s_ctx)
            x = _even_out(o, gb, ya, norm_w, w_out_e, j, x, gt, lg, lb)
        else:
            ps = pool_scale[j][None, :]
            if ctx_needed:
                m_c = _odd_mix(ctx_s, sh_c, sc_c, w_in_o, w_pool, j, ps, 1)
                ctx_s = _out_ln(m_c, w_out_o, j, ctx_s, gt_c, lg, lb)
            m = _odd_mix(x, sh, sc, w_in_o, w_pool, j, ps, GRID_W)
            x = _out_ln(m, w_out_o, j, x, gt, lg, lb)
    return x
```

```python
import functools

import jax
import jax.numpy as jnp
from jax import lax
from jax.experimental import pallas as pl
from jax.experimental.pallas import tpu as pltpu

D_MODEL = 2048
DEPTH = 4
GRID_W = 64

GLA_HEADS = 4
GLA_DK = 256
GLA_DV = 256
GLA_KEY_W = GLA_HEADS * GLA_DK
GLA_VAL_W = GLA_HEADS * GLA_DV
GLA_GATE_RANK = 16
GLA_TAU = 16.0
CONV_W = D_MODEL // 2
POOL_W = D_MODEL
POOL_WINDOWS = (2, 4, 8, 16)
POOL_GROUP = POOL_W // len(POOL_WINDOWS)
POOL_HALO_ROWS = max(POOL_WINDOWS) // 2
LN_EPS = 1e-5
ALPHA = (2.0 * DEPTH) ** 0.25

LANES = 128
COND_ROWS = 8
VMEM_LIMIT = 60000 * 1024
GLA_CHUNK = 256
GLA_BLOCK_CHUNKS = 4
ROW_TILE = 256
SUB_TILE = 256
EVEN_OUT_TILE = 512
OUT_LN_TILE = 1024
POOL_TILE = 512
RANK_COLS = 2 * GLA_GATE_RANK
STATE_W = GLA_KEY_W + GLA_VAL_W
LOG2_E = 1.4426950408889634

F32 = jnp.float32
BF16 = jnp.bfloat16
HIGHEST = lax.Precision.HIGHEST

_NT = (((1,), (1,)), ((), ()))
_TN = (((0,), (0,)), ((), ()))


def _silu(t):
    return t / (1.0 + jnp.exp(-t))


def _ordering_zero(t):
    bits = lax.bitcast_convert_type(t, jnp.uint32)
    return lax.bitcast_convert_type((bits >> 16) >> 16, F32)


def _resident(shape):
    return pl.BlockSpec(shape, lambda *_: (0,) * len(shape), pipeline_mode=pl.Buffered(1))


def _layer_resident(stacked_shape, layer):
    nd = len(stacked_shape)
    return pl.BlockSpec((None,) + tuple(stacked_shape[1:]), lambda *_: (layer,) + (0,) * (nd - 1),
                        pipeline_mode=pl.Buffered(1))


def _params(*sem):
    return pltpu.CompilerParams(dimension_semantics=sem, vmem_limit_bytes=VMEM_LIMIT)


def _modulation_kernel(c_ref, w_ref, b_ref, o_ref):
    s = _silu(c_ref[...])
    o_ref[...] = jnp.dot(s, w_ref[...], precision=HIGHEST, preferred_element_type=F32) + b_ref[...]


def _modulation(cond, w_ada, b_ada):
    depth, d, n = w_ada.shape
    tn = 768
    return pl.pallas_call(
        _modulation_kernel,
        grid=(depth, n // tn),
        in_specs=[
            pl.BlockSpec((COND_ROWS, d), lambda i, j: (0, 0)),
            pl.BlockSpec((None, d, tn), lambda i, j: (i, 0, j)),
            pl.BlockSpec((None, 1, tn), lambda i, j: (i, 0, j)),
        ],
        out_specs=pl.BlockSpec((None, COND_ROWS, tn), lambda i, j: (i, 0, j)),
        out_shape=jax.ShapeDtypeStruct((depth, COND_ROWS, n), F32),
        compiler_params=_params("parallel", "parallel"),
        name="modulation",
    )(cond, w_ada, b_ada.reshape(depth, 1, n))


def _even_in_kernel(x_ref, sh_ref, sc_ref, w_ref, wr_ref, cw_ref,
                    kvq_ref, gb_ref, ya_ref, r_ref, *, row_w):
    tm = x_ref.shape[0]
    u = (x_ref[...] * (1.0 + sc_ref[...]) + sh_ref[...]).astype(BF16)

    def proj(col):
        return jnp.dot(u, w_ref[:, col:col + GLA_KEY_W], preferred_element_type=F32)

    kvq_ref[:, 0:1024] = proj(0).astype(BF16)
    kvq_ref[:, 1024:2048] = proj(1024).astype(BF16)
    kvq_ref[:, 2048:3072] = (proj(2048) * GLA_DK ** -0.5).astype(BF16)
    gb_ref[...] = _silu(proj(3072)).astype(BF16)

    r = jnp.dot(u, wr_ref[...], preferred_element_type=F32)
    r_hi = r.astype(BF16)
    r_lo = (r - r_hi.astype(F32)).astype(BF16)
    lane = lax.broadcasted_iota(jnp.int32, r.shape, 1)
    r_ref[...] = jnp.where((lane >= RANK_COLS) & (lane < 2 * RANK_COLS), r_lo, r_hi)

    p = proj(5120) * proj(6144)
    t = lax.broadcasted_iota(jnp.int32, (tm, 1), 0) % row_w
    prev = jnp.where(t == 0, 0.0, pltpu.roll(p, 1, axis=0))
    nxt = jnp.where(t == row_w - 1, 0.0, pltpu.roll(p, tm - 1, axis=0))
    conv = cw_ref[0:1, :] * prev + cw_ref[1:2, :] * p + cw_ref[2:3, :] * nxt
    ya_ref[...] = (proj(4096) * conv * _silu(proj(7168))).astype(BF16)


def _even_in(x, sh, sc, w_main, w_r, layer, conv_w, row_w):
    bn, L, d = x.shape
    tm = min(ROW_TILE, L)
    assert L % tm == 0 and tm % row_w == 0
    tile = lambda n: pl.BlockSpec((None, tm, n), lambda b, i: (b, i, 0))
    vec = pl.BlockSpec((None, 1, d), lambda b, i: (b, 0, 0))
    return pl.pallas_call(
        functools.partial(_even_in_kernel, row_w=row_w),
        grid=(bn, L // tm),
        in_specs=[tile(d), vec, vec, _layer_resident(w_main.shape, layer),
                  _layer_resident(w_r.shape, layer),
                  _resident(conv_w.shape)],
        out_specs=[tile(3 * GLA_KEY_W), tile(GLA_VAL_W), tile(CONV_W), tile(LANES)],
        out_shape=[jax.ShapeDtypeStruct((bn, L, 3 * GLA_KEY_W), BF16),
                   jax.ShapeDtypeStruct((bn, L, GLA_VAL_W), BF16),
                   jax.ShapeDtypeStruct((bn, L, CONV_W), BF16),
                   jax.ShapeDtypeStruct((bn, L, LANES), BF16)],
        compiler_params=_params("parallel", "parallel"),
        name="even_in",
    )(x, sh, sc, w_main, w_r, conv_w)


def _gla_prologue(q, k, r, wg, bg, tri2, fwd):
    chunk = q.shape[0]
    z2 = jnp.dot(r, wg, preferred_element_type=F32) + bg
    ls2 = jnp.minimum(z2, 0.0) - jnp.log2(1.0 + jnp.exp2(-jnp.abs(z2)))
    ls_hi = ls2.astype(BF16)
    ls_lo = (ls2 - ls_hi.astype(F32)).astype(BF16)
    g = jnp.dot(tri2, jnp.concatenate([ls_hi, ls_lo], axis=0), preferred_element_type=F32)
    p = g[chunk // 2:chunk // 2 + 1, :]
    g_tot = jnp.where(fwd, g[chunk - 1:chunk, :], g[0:1, :])
    qg = q * jnp.exp2(g - p).astype(BF16)
    kg = k * jnp.exp2(p - g).astype(BF16)
    kd = kg * jnp.exp2(g_tot - p).astype(BF16)
    return qg, kg, kd, jnp.exp2(p), jnp.exp2(g_tot)


def _gla_heads(factors, v_ref, o_ref, rows, tri, st_scr):
    qg_all, kg_all, kd_all, e_p, e_tot = factors
    first = None
    for h in range(GLA_HEADS):
        hs = slice(h * GLA_DK, (h + 1) * GLA_DK)
        qg = qg_all[:, hs]
        v = v_ref[rows, hs]
        a = lax.dot_general(qg, kg_all[:, hs], _NT, preferred_element_type=F32)
        if first is None:
            first = a[0:COND_ROWS, 0:LANES]
        a = jnp.where(tri, a, 0.0).astype(BF16)
        st = st_scr[h]
        o = jnp.dot(a, v, preferred_element_type=F32)
        o += lax.dot_general(qg, (st * e_p[:, hs]).astype(BF16), _NT, preferred_element_type=F32)
        o_ref[rows, hs] = o
        st_scr[h] = st * e_tot[:, hs] + lax.dot_general(v, kd_all[:, hs], _TN,
                                                        preferred_element_type=F32)
    return first


def _gla_kernel(k_ref, v_ref, q_ref, r_ref, wg_ref, bg_ref, s0_ref, o_ref, sfin_ref, st_scr,
                *, chunk):
    d = pl.program_id(0)
    c = pl.program_id(2)
    n_sub = k_ref.shape[0] // chunk

    @pl.when(c == 0)
    def _():
        st_scr[...] = s0_ref[...]

    row = lax.broadcasted_iota(jnp.int32, (chunk, chunk), 0)
    col = lax.broadcasted_iota(jnp.int32, (chunk, chunk), 1)
    fwd = d == 0
    tri = jnp.where(fwd, row - col, col - row) >= 0
    tri_b = jnp.where(tri, 1.0 / GLA_TAU, 0.0).astype(BF16)
    tri2 = jnp.concatenate([tri_b, tri_b], axis=1)

    def rows_of(s):
        pos = s + d * (n_sub - 1 - 2 * s)
        return pl.ds(pl.multiple_of(pos * chunk, chunk), chunk)

    def prologue(s, bg):
        rows = rows_of(s)
        return _gla_prologue(q_ref[rows, :], k_ref[rows, :], r_ref[rows, :], wg_ref[...], bg,
                             tri2, fwd)

    bg = bg_ref[...]
    factors = prologue(0, bg)
    for s in range(n_sub):
        first = _gla_heads(factors, v_ref, o_ref, rows_of(s), tri, st_scr)
        if s + 1 < n_sub:
            zero = _ordering_zero(first)[0:1, :]
            factors = prologue(s + 1, bg + jnp.tile(zero, (1, GLA_KEY_W // LANES)))

    @pl.when(c == pl.num_programs(2) - 1)
    def _():
        sfin_ref[...] = st_scr[...]


def _gla(kvq, r3, wg3, bg, s0):
    bn, L, _ = kvq.shape
    chunk = min(GLA_CHUNK, L)
    blk = min(GLA_BLOCK_CHUNKS * chunk, L)
    nc = L // blk
    assert L % blk == 0 and blk % chunk == 0

    def cidx(d, c):
        return c + d * (nc - 1 - 2 * c)

    def part(n):
        return pl.BlockSpec((None, blk, GLA_KEY_W), lambda d, b, c: (b, cidx(d, c), n))

    state = pl.BlockSpec((None, None, GLA_HEADS, GLA_DV, GLA_DK), lambda d, b, c: (d, b, 0, 0, 0))
    return pl.pallas_call(
        functools.partial(_gla_kernel, chunk=chunk),
        grid=(2, bn, nc),
        in_specs=[part(0), part(1), part(2),
                  pl.BlockSpec((None, blk, LANES), lambda d, b, c: (b, cidx(d, c), 0)),
                  pl.BlockSpec((None, LANES, GLA_KEY_W), lambda d, b, c: (d, 0, 0)),
                  pl.BlockSpec((None, 1, GLA_KEY_W), lambda d, b, c: (d, 0, 0)),
                  state],
        out_specs=[pl.BlockSpec((None, None, blk, GLA_VAL_W), lambda d, b, c: (d, b, cidx(d, c), 0)),
                   state],
        out_shape=[jax.ShapeDtypeStruct((2, bn, L, GLA_VAL_W), F32),
                   jax.ShapeDtypeStruct((2, bn, GLA_HEADS, GLA_DV, GLA_DK), F32)],
        scratch_shapes=[pltpu.VMEM((GLA_HEADS, GLA_DV, GLA_DK), F32)],
        compiler_params=_params("parallel", "parallel", "arbitrary"),
        name="gla",
    )(kvq, kvq, kvq, r3, wg3, bg, s0)


def _residual_ln(x, y, gt, g, b):
    xn = ALPHA * x + gt * y
    mu = jnp.mean(xn, axis=-1, keepdims=True)
    xc = xn - mu
    var = jnp.mean(xc * xc, axis=-1, keepdims=True)
    return xc * lax.rsqrt(var + LN_EPS) * g + b


def _gla_branch(of_ref, ob_ref, gb_ref, nw_ref, rows):
    o = of_ref[rows, :] + ob_ref[rows, :]
    heads = []
    for h in range(GLA_HEADS):
        oh = o[:, h * GLA_DV:(h + 1) * GLA_DV]
        ms = jnp.mean(oh * oh, axis=-1, keepdims=True)
        heads.append(oh * lax.rsqrt(ms + LN_EPS) * nw_ref[...])
    return (jnp.concatenate(heads, axis=-1) * gb_ref[rows, :].astype(F32)).astype(BF16)


def _even_out_kernel(of_ref, ob_ref, gb_ref, ya_ref, nw_ref, w_ref, x_ref, gt_ref, lg_ref, lb_ref,
                     out_ref):
    n_sub = x_ref.shape[0] // SUB_TILE
    sub = lambda s: pl.ds(s * SUB_TILE, SUB_TILE)
    yb = _gla_branch(of_ref, ob_ref, gb_ref, nw_ref, sub(0))
    y = None
    for s in range(n_sub + 1):
        y_prev = y
        if s < n_sub:
            y = jnp.dot(yb, w_ref[0:GLA_VAL_W, :], preferred_element_type=F32)
            y += jnp.dot(ya_ref[sub(s), :], w_ref[GLA_VAL_W:, :], preferred_element_type=F32)
        if s + 1 < n_sub:
            yb = _gla_branch(of_ref, ob_ref, gb_ref, nw_ref, sub(s + 1))
        if s > 0:
            out_ref[sub(s - 1), :] = _residual_ln(x_ref[sub(s - 1), :], y_prev, gt_ref[...],
                                                  lg_ref[...], lb_ref[...])


def _even_out(o, gb, ya, norm_w, w_out, layer, x, gt, ln_g, ln_b):
    bn, L, d = x.shape
    tm = min(EVEN_OUT_TILE, L)
    assert L % tm == 0 and tm % SUB_TILE == 0
    tile = lambda n: pl.BlockSpec((None, tm, n), lambda b, i: (b, i, 0))
    odir = lambda k: pl.BlockSpec((None, None, tm, GLA_VAL_W), lambda b, i: (k, b, i, 0))
    return pl.pallas_call(
        _even_out_kernel,
        grid=(bn, L // tm),
        in_specs=[odir(0), odir(1), tile(GLA_VAL_W), tile(CONV_W), _resident(norm_w.shape),
                  _layer_resident(w_out.shape, layer), tile(d),
                  pl.BlockSpec((None, 1, d), lambda b, i: (b, 0, 0)),
                  _resident(ln_g.shape), _resident(ln_b.shape)],
        out_specs=tile(d),
        out_shape=jax.ShapeDtypeStruct((bn, L, d), F32),
        compiler_params=_params("parallel", "parallel"),
        name="even_out",
    )(o, o, gb, ya, norm_w, w_out, x, gt, ln_g, ln_b)


def _out_ln_kernel(m_ref, w_ref, x_ref, gt_ref, lg_ref, lb_ref, out_ref):
    n_sub = x_ref.shape[0] // SUB_TILE
    sub = lambda s: pl.ds(s * SUB_TILE, SUB_TILE)
    y = None
    for s in range(n_sub + 1):
        y_prev = y
        if s < n_sub:
            y = jnp.dot(m_ref[sub(s), :], w_ref[...], preferred_element_type=F32)
        if s > 0:
            out_ref[sub(s - 1), :] = _residual_ln(x_ref[sub(s - 1), :], y_prev, gt_ref[...],
                                                  lg_ref[...], lb_ref[...])


def _out_ln(m, w_out, layer, x, gt, ln_g, ln_b):
    bn, L, d = x.shape
    tm = min(OUT_LN_TILE, L)
    assert L % tm == 0 and tm % SUB_TILE == 0
    tile = lambda n: pl.BlockSpec((None, tm, n), lambda b, i: (b, i, 0))
    return pl.pallas_call(
        _out_ln_kernel,
        grid=(bn, L // tm),
        in_specs=[tile(m.shape[-1]), _layer_resident(w_out.shape, layer), tile(d),
                  pl.BlockSpec((None, 1, d), lambda b, i: (b, 0, 0)),
                  _resident(ln_g.shape), _resident(ln_b.shape)],
        out_specs=tile(d),
        out_shape=jax.ShapeDtypeStruct((bn, L, d), F32),
        compiler_params=_params("parallel", "parallel"),
        name="out_ln",
    )(m, w_out, x, gt, ln_g, ln_b)


def _odd_mix_kernel(xp_ref, xc_ref, xn_ref, sh_ref, sc_ref, win_ref, wpool_ref, ps_ref, m_ref,
                    *, row_stride, n_rows):
    i = pl.program_id(1)
    tm = xc_ref.shape[0]
    halo = POOL_HALO_ROWS * row_stride
    xc = xc_ref[...]
    scale1 = 1.0 + sc_ref[...]
    u = (xc * scale1 + sh_ref[...]).astype(BF16)
    sgate = _silu(jnp.dot(u, win_ref[:, POOL_W:], preferred_element_type=F32)).astype(BF16)

    before = jnp.where(i > 0, xp_ref[tm - halo:, :], 0.0)
    after = jnp.where(i < pl.num_programs(1) - 1, xn_ref[:halo, :], 0.0)
    xe = jnp.concatenate([before, xc, after], axis=0)
    grid_row = (i * tm + lax.broadcasted_iota(jnp.int32, (tm, 1), 0)) // row_stride

    def window_diff(s, w):
        half = w // 2
        s = s[:s.shape[0] - half * row_stride, :] + s[half * row_stride:, :]
        start = halo - half * row_stride
        cnt = jnp.minimum(grid_row + half, n_rows) - jnp.maximum(grid_row - half, 0)
        mean = s[start:start + tm, :] / cnt.astype(F32)
        return s, ((mean - xc) * scale1).astype(BF16)

    s, diff = window_diff(xe, POOL_WINDOWS[0])
    for g in range(len(POOL_WINDOWS)):
        cur = diff
        if g + 1 < len(POOL_WINDOWS):
            s, diff = window_diff(s, POOL_WINDOWS[g + 1])
        cols = slice(g * POOL_GROUP, (g + 1) * POOL_GROUP)
        z = jnp.dot(cur, win_ref[:, cols], preferred_element_type=F32)
        z = jnp.dot(z.astype(BF16), wpool_ref[g], preferred_element_type=F32)
        m_ref[:, cols] = (z * ps_ref[:, cols] * sgate[:, cols].astype(F32)).astype(BF16)


def _odd_mix(x, sh, sc, w_in, w_pool, layer, pool_scale, row_stride):
    bn, L, d = x.shape
    n_rows = L // row_stride
    tm = min(POOL_TILE, L)
    nt = L // tm
    assert L % tm == 0 and tm % row_stride == 0 and POOL_HALO_ROWS * row_stride <= tm
    vec = pl.BlockSpec((None, 1, d), lambda b, i: (b, 0, 0))
    return pl.pallas_call(
        functools.partial(_odd_mix_kernel, row_stride=row_stride, n_rows=n_rows),
        grid=(bn, nt),
        in_specs=[pl.BlockSpec((None, tm, d), lambda b, i: (b, jnp.maximum(i - 1, 0), 0)),
                  pl.BlockSpec((None, tm, d), lambda b, i: (b, i, 0)),
                  pl.BlockSpec((None, tm, d), lambda b, i: (b, jnp.minimum(i + 1, nt - 1), 0)),
                  vec, vec, _layer_resident(w_in.shape, layer), _layer_resident(w_pool.shape, layer),
                  _resident(pool_scale.shape)],
        out_specs=pl.BlockSpec((None, tm, POOL_W), lambda b, i: (b, i, 0)),
        out_shape=jax.ShapeDtypeStruct((bn, L, POOL_W), BF16),
        compiler_params=_params("parallel", "parallel"),
        name="odd_mix",
    )(x, x, x, sh, sc, w_in, w_pool, pool_scale)


def _hi_lo(w):
    hi = w.astype(BF16)
    return hi, (w - hi.astype(F32)).astype(BF16)


def _even_w_prep_kernel(w_ref, main_ref, r_ref):
    w = w_ref[...]
    main_ref[:, 0:STATE_W] = w[:, 0:STATE_W].astype(BF16)
    main_ref[:, STATE_W:] = w[:, STATE_W + RANK_COLS:].astype(BF16)
    rank = w[:, STATE_W:STATE_W + RANK_COLS].astype(BF16)
    pad = jnp.zeros((w.shape[0], LANES - 3 * RANK_COLS), BF16)
    r_ref[...] = jnp.concatenate([rank, rank, rank, pad], axis=1)


def _even_w_prep(w_in_e):
    ne, d, n = w_in_e.shape
    tr = 128
    return pl.pallas_call(
        _even_w_prep_kernel,
        grid=(ne, d // tr),
        in_specs=[pl.BlockSpec((None, tr, n), lambda j, i: (j, i, 0))],
        out_specs=[pl.BlockSpec((None, tr, n - RANK_COLS), lambda j, i: (j, i, 0)),
                   pl.BlockSpec((None, tr, LANES), lambda j, i: (j, i, 0))],
        out_shape=[jax.ShapeDtypeStruct((ne, d, n - RANK_COLS), BF16),
                   jax.ShapeDtypeStruct((ne, d, LANES), BF16)],
        compiler_params=_params("parallel", "parallel"),
        name="even_w_prep",
    )(w_in_e)


def _gate_weights(w_gf, b_gf, w_gb, b_gb):
    zr = jnp.zeros((GLA_GATE_RANK, GLA_KEY_W), BF16)
    tail = jnp.zeros((LANES - 3 * RANK_COLS, GLA_KEY_W), BF16)
    f_hi, f_lo = _hi_lo(w_gf * LOG2_E)
    b_hi, b_lo = _hi_lo(w_gb * LOG2_E)
    wg3 = jnp.stack([jnp.concatenate([f_hi, zr, f_hi, zr, f_lo, zr, tail], axis=0),
                     jnp.concatenate([zr, b_hi, zr, b_hi, zr, b_lo, tail], axis=0)])
    bg = jnp.stack([b_gf, b_gb])[:, None, :] * LOG2_E
    return wg3, bg


def kernel(x, c, ctx, c_ctx, w_ada, b_ada, ln_g, ln_b, w_in_e, w_gate_f, b_gate_f, w_gate_b,
           b_gate_b, gla_norm_w, conv_w, w_out_e, w_in_o, w_pool, pool_scale, w_out_o):
    bn, L, d = x.shape
    lc = ctx.shape[1]
    cond = jnp.concatenate([c, c_ctx[None, :], jnp.zeros((COND_ROWS - bn - 1, d), F32)], axis=0)
    mod = _modulation(cond, w_ada, b_ada)
    s_zero = jnp.zeros((2, bn, GLA_HEADS, GLA_DV, GLA_DK), F32)
    w_main, w_r = _even_w_prep(w_in_e)
    w_out_e, w_in_o, w_out_o, w_pool = (
        w.astype(BF16) for w in (w_out_e, w_in_o, w_out_o, w_pool))
    ctx_s = ctx
    for i in range(DEPTH):
        j = i // 2
        ctx_needed = any(l % 2 == 0 for l in range(i + 1, DEPTH))
        sh, sc, gt = (mod[i, :bn, n * d:(n + 1) * d][:, None, :] for n in range(3))
        sh_c, sc_c, gt_c = (jnp.broadcast_to(mod[i, bn, n * d:(n + 1) * d], (bn, 1, d))
                            for n in range(3))
        lg, lb = ln_g[i][None, :], ln_b[i][None, :]
        if i % 2 == 0:
            wg3, bg = _gate_weights(w_gate_f[j], b_gate_f[j], w_gate_b[j], b_gate_b[j])
            norm_w = gla_norm_w[j][None, :]
            kvq_c, gb_c, ya_c, r_c = _even_in(ctx_s, sh_c, sc_c, w_main, w_r, j, conv_w[j], lc)
            o_c, s_ctx = _gla(kvq_c, r_c, wg3, bg, s_zero)
            if ctx_needed:
                ctx_s = _even_out(o_c, gb_c, ya_c, norm_w, w_out_e, j, ctx_s, gt_c, lg, lb)
            kvq, gb, ya, r = _even_in(x, sh, sc, w_main, w_r, j, conv_w[j], GRID_W)
            o, _ = _gla(kvq, r, wg3, bg, s_ctx)
            x = _even_out(o, gb, ya, norm_w, w_out_e, j, x, gt, lg, lb)
        else:
            ps = pool_scale[j][None, :]
            if ctx_needed:
                m_c = _odd_mix(ctx_s, sh_c, sc_c, w_in_o, w_pool, j, ps, 1)
                ctx_s = _out_ln(m_c, w_out_o, j, ctx_s, gt_c, lg, lb)
            m = _odd_mix(x, sh, sc, w_in_o, w_pool, j, ps, GRID_W)
            x = _out_ln(m, w_out_o, j, x, gt, lg, lb)
    return x
```

```python
import functools

import jax
import jax.numpy as jnp
from jax import lax
from jax.experimental import pallas as pl
from jax.experimental.pallas import tpu as pltpu

D_MODEL = 2048
DEPTH = 4
GRID_W = 64

GLA_HEADS = 4
GLA_DK = 256
GLA_DV = 256
GLA_KEY_W = GLA_HEADS * GLA_DK
GLA_VAL_W = GLA_HEADS * GLA_DV
GLA_GATE_RANK = 16
GLA_TAU = 16.0
CONV_W = D_MODEL // 2
POOL_W = D_MODEL
POOL_WINDOWS = (2, 4, 8, 16)
POOL_GROUP = POOL_W // len(POOL_WINDOWS)
POOL_HALO_ROWS = max(POOL_WINDOWS) // 2
LN_EPS = 1e-5
ALPHA = (2.0 * DEPTH) ** 0.25

LANES = 128
COND_ROWS = 8
VMEM_LIMIT = 60000 * 1024
GLA_CHUNK = 256
GLA_BLOCK_CHUNKS = 4
ROW_TILE = 256
SUB_TILE = 256
EVEN_OUT_TILE = 512
OUT_LN_TILE = 1024
LN_ROWS = 8
NORM_ROWS = 16
POOL_TILE = 512
RANK_COLS = 2 * GLA_GATE_RANK
STATE_W = GLA_KEY_W + GLA_VAL_W
LOG2_E = 1.4426950408889634

F32 = jnp.float32
BF16 = jnp.bfloat16

_NT = (((1,), (1,)), ((), ()))
_TN = (((0,), (0,)), ((), ()))


def _silu(t):
    return t / (1.0 + jnp.exp(-t))


def _ordering_zero(t):
    bits = lax.bitcast_convert_type(t, jnp.uint32)
    return lax.bitcast_convert_type((bits >> 16) >> 16, F32)


def _resident(shape):
    return pl.BlockSpec(shape, lambda *_: (0,) * len(shape), pipeline_mode=pl.Buffered(1))


def _layer_resident(stacked_shape, layer):
    nd = len(stacked_shape)
    return pl.BlockSpec((None,) + tuple(stacked_shape[1:]), lambda *_: (layer,) + (0,) * (nd - 1),
                        pipeline_mode=pl.Buffered(1))


def _params(*sem):
    return pltpu.CompilerParams(dimension_semantics=sem, vmem_limit_bytes=VMEM_LIMIT)


def _modulation_kernel(c_ref, w_ref, b_ref, o_ref):
    s = _silu(c_ref[...]).astype(BF16)
    o_ref[...] = jnp.dot(s, w_ref[...].astype(BF16), preferred_element_type=F32) + b_ref[...]


def _modulation(cond, w_ada, b_ada):
    depth, d, n = w_ada.shape
    tn = 768
    return pl.pallas_call(
        _modulation_kernel,
        grid=(depth, n // tn),
        in_specs=[
            pl.BlockSpec((COND_ROWS, d), lambda i, j: (0, 0)),
            pl.BlockSpec((None, d, tn), lambda i, j: (i, 0, j)),
            pl.BlockSpec((None, 1, tn), lambda i, j: (i, 0, j)),
        ],
        out_specs=pl.BlockSpec((None, COND_ROWS, tn), lambda i, j: (i, 0, j)),
        out_shape=jax.ShapeDtypeStruct((depth, COND_ROWS, n), F32),
        compiler_params=_params("parallel", "parallel"),
        name="modulation",
    )(cond, w_ada, b_ada.reshape(depth, 1, n))


def _even_in_kernel(x_ref, sh_ref, sc_ref, w_ref, wr_ref, cw_ref,
                    kvq_ref, gb_ref, ya_ref, r_ref, *, row_w):
    tm = x_ref.shape[0]
    u = (x_ref[...] * (1.0 + sc_ref[...]) + sh_ref[...]).astype(BF16)

    def proj(col):
        return jnp.dot(u, w_ref[:, col:col + GLA_KEY_W], preferred_element_type=F32)

    kvq_ref[:, 0:1024] = proj(0).astype(BF16)
    kvq_ref[:, 1024:2048] = proj(1024).astype(BF16)
    kvq_ref[:, 2048:3072] = (proj(2048) * GLA_DK ** -0.5).astype(BF16)
    gb_ref[...] = _silu(proj(3072)).astype(BF16)

    r = jnp.dot(u, wr_ref[...], preferred_element_type=F32)
    r_hi = r.astype(BF16)
    r_lo = (r - r_hi.astype(F32)).astype(BF16)
    lane = lax.broadcasted_iota(jnp.int32, r.shape, 1)
    r_ref[...] = jnp.where((lane >= RANK_COLS) & (lane < 2 * RANK_COLS), r_lo, r_hi)

    p = proj(5120) * proj(6144)
    t = lax.broadcasted_iota(jnp.int32, (tm, 1), 0) % row_w
    prev = jnp.where(t == 0, 0.0, pltpu.roll(p, 1, axis=0))
    nxt = jnp.where(t == row_w - 1, 0.0, pltpu.roll(p, tm - 1, axis=0))
    conv = cw_ref[0:1, :] * prev + cw_ref[1:2, :] * p + cw_ref[2:3, :] * nxt
    ya_ref[...] = (proj(4096) * conv * _silu(proj(7168))).astype(BF16)


def _even_in(x, sh, sc, w_main, w_r, layer, conv_w, row_w):
    bn, L, d = x.shape
    tm = min(ROW_TILE, L)
    assert L % tm == 0 and tm % row_w == 0
    tile = lambda n: pl.BlockSpec((None, tm, n), lambda b, i: (b, i, 0))
    vec = pl.BlockSpec((None, 1, d), lambda b, i: (b, 0, 0))
    return pl.pallas_call(
        functools.partial(_even_in_kernel, row_w=row_w),
        grid=(bn, L // tm),
        in_specs=[tile(d), vec, vec, _layer_resident(w_main.shape, layer),
                  _layer_resident(w_r.shape, layer),
                  _resident(conv_w.shape)],
        out_specs=[tile(3 * GLA_KEY_W), tile(GLA_VAL_W), tile(CONV_W), tile(LANES)],
        out_shape=[jax.ShapeDtypeStruct((bn, L, 3 * GLA_KEY_W), BF16),
                   jax.ShapeDtypeStruct((bn, L, GLA_VAL_W), BF16),
                   jax.ShapeDtypeStruct((bn, L, CONV_W), BF16),
                   jax.ShapeDtypeStruct((bn, L, LANES), BF16)],
        compiler_params=_params("parallel", "parallel"),
        name="even_in",
    )(x, sh, sc, w_main, w_r, conv_w)


def _gla_prologue(q, k, r, wg, bg, tri2, fwd):
    chunk = q.shape[0]
    z2 = jnp.dot(r, wg, preferred_element_type=F32) + bg
    ls2 = jnp.minimum(z2, 0.0) - jnp.log2(1.0 + jnp.exp2(-jnp.abs(z2)))
    ls_hi = ls2.astype(BF16)
    ls_lo = (ls2 - ls_hi.astype(F32)).astype(BF16)
    g = jnp.dot(tri2, jnp.concatenate([ls_hi, ls_lo], axis=0), preferred_element_type=F32)
    p = g[chunk // 2:chunk // 2 + 1, :]
    g_tot = jnp.where(fwd, g[chunk - 1:chunk, :], g[0:1, :])
    qg = q * jnp.exp2(g - p).astype(BF16)
    kg = k * jnp.exp2(p - g).astype(BF16)
    kd = kg * jnp.exp2(g_tot - p).astype(BF16)
    return qg, kg, kd, jnp.exp2(p), jnp.exp2(g_tot)


def _gla_heads(factors, v_ref, o_ref, rows, tri, st_scr):
    qg_all, kg_all, kd_all, e_p, e_tot = factors
    first = None
    for h in range(GLA_HEADS):
        hs = slice(h * GLA_DK, (h + 1) * GLA_DK)
        qg = qg_all[:, hs]
        v = v_ref[rows, hs]
        a = lax.dot_general(qg, kg_all[:, hs], _NT, preferred_element_type=F32)
        if first is None:
            first = a[0:COND_ROWS, 0:LANES]
        a = jnp.where(tri, a, 0.0).astype(BF16)
        st = st_scr[h]
        o = jnp.dot(a, v, preferred_element_type=F32)
        o += lax.dot_general(qg, (st * e_p[:, hs]).astype(BF16), _NT, preferred_element_type=F32)
        o_ref[rows, hs] = o
        st_scr[h] = st * e_tot[:, hs] + lax.dot_general(v, kd_all[:, hs], _TN,
                                                        preferred_element_type=F32)
    return first


def _gla_kernel(k_ref, v_ref, q_ref, r_ref, wg_ref, bg_ref, s0_ref, o_ref, sfin_ref, st_scr,
                *, chunk):
    d = pl.program_id(0)
    c = pl.program_id(2)
    n_sub = k_ref.shape[0] // chunk

    @pl.when(c == 0)
    def _():
        st_scr[...] = s0_ref[...]

    row = lax.broadcasted_iota(jnp.int32, (chunk, chunk), 0)
    col = lax.broadcasted_iota(jnp.int32, (chunk, chunk), 1)
    fwd = d == 0
    tri = jnp.where(fwd, row - col, col - row) >= 0
    tri_b = jnp.where(tri, 1.0 / GLA_TAU, 0.0).astype(BF16)
    tri2 = jnp.concatenate([tri_b, tri_b], axis=1)

    def rows_of(s):
        pos = s + d * (n_sub - 1 - 2 * s)
        return pl.ds(pl.multiple_of(pos * chunk, chunk), chunk)

    def prologue(s, bg):
        rows = rows_of(s)
        return _gla_prologue(q_ref[rows, :], k_ref[rows, :], r_ref[rows, :], wg_ref[...], bg,
                             tri2, fwd)

    bg = bg_ref[...]
    factors = prologue(0, bg)
    for s in range(n_sub):
        first = _gla_heads(factors, v_ref, o_ref, rows_of(s), tri, st_scr)
        if s + 1 < n_sub:
            zero = _ordering_zero(first)[0:1, :]
            factors = prologue(s + 1, bg + jnp.tile(zero, (1, GLA_KEY_W // LANES)))

    @pl.when(c == pl.num_programs(2) - 1)
    def _():
        sfin_ref[...] = st_scr[...]


def _gla(kvq, r3, wg3, bg, s0):
    bn, L, _ = kvq.shape
    chunk = min(GLA_CHUNK, L)
    blk = min(GLA_BLOCK_CHUNKS * chunk, L)
    nc = L // blk
    assert L % blk == 0 and blk % chunk == 0

    def cidx(d, c):
        return c + d * (nc - 1 - 2 * c)

    def part(n):
        return pl.BlockSpec((None, blk, GLA_KEY_W), lambda d, b, c: (b, cidx(d, c), n))

    state = pl.BlockSpec((None, None, GLA_HEADS, GLA_DV, GLA_DK), lambda d, b, c: (d, b, 0, 0, 0))
    return pl.pallas_call(
        functools.partial(_gla_kernel, chunk=chunk),
        grid=(2, bn, nc),
        in_specs=[part(0), part(1), part(2),
                  pl.BlockSpec((None, blk, LANES), lambda d, b, c: (b, cidx(d, c), 0)),
                  pl.BlockSpec((None, LANES, GLA_KEY_W), lambda d, b, c: (d, 0, 0)),
                  pl.BlockSpec((None, 1, GLA_KEY_W), lambda d, b, c: (d, 0, 0)),
                  state],
        out_specs=[pl.BlockSpec((None, None, blk, GLA_VAL_W), lambda d, b, c: (d, b, cidx(d, c), 0)),
                   state],
        out_shape=[jax.ShapeDtypeStruct((2, bn, L, GLA_VAL_W), F32),
                   jax.ShapeDtypeStruct((2, bn, GLA_HEADS, GLA_DV, GLA_DK), F32)],
        scratch_shapes=[pltpu.VMEM((GLA_HEADS, GLA_DV, GLA_DK), F32)],
        compiler_params=_params("parallel", "parallel", "arbitrary"),
        name="gla",
    )(kvq, kvq, kvq, r3, wg3, bg, s0)


def _residual_ln_rows(x_ref, out_ref, row0, y, gts, g, b):
    for r in range(0, y.shape[0], LN_ROWS):
        rows = pl.ds(row0 + r, LN_ROWS)
        w = x_ref[rows, :] + gts * y[r:r + LN_ROWS, :]
        mu = jnp.mean(w, axis=-1, keepdims=True)
        xc = w - mu
        var = jnp.mean(xc * xc, axis=-1, keepdims=True)
        out_ref[rows, :] = xc * lax.rsqrt(var + LN_EPS / ALPHA ** 2) * g + b


def _gla_branch_rows(of_ref, ob_ref, gb_ref, nw, row0, yb_ref):
    for r in range(0, SUB_TILE, NORM_ROWS):
        rows = pl.ds(row0 + r, NORM_ROWS)
        o = of_ref[rows, :] + ob_ref[rows, :]
        heads = []
        for h in range(GLA_HEADS):
            oh = o[:, h * GLA_DV:(h + 1) * GLA_DV]
            ms = jnp.mean(oh * oh, axis=-1, keepdims=True)
            heads.append(oh * lax.rsqrt(ms + LN_EPS) * nw)
        yb = jnp.concatenate(heads, axis=-1) * gb_ref[rows, :].astype(F32)
        yb_ref[r:r + NORM_ROWS, :] = yb.astype(BF16)


def _even_out_kernel(of_ref, ob_ref, gb_ref, ya_ref, nw_ref, w_ref, x_ref, gt_ref, lg_ref, lb_ref,
                     out_ref, yb_scr):
    n_sub = x_ref.shape[0] // SUB_TILE
    sub = lambda s: pl.ds(s * SUB_TILE, SUB_TILE)
    gts = gt_ref[...] * (1.0 / ALPHA)
    nw, lg, lb = nw_ref[...], lg_ref[...], lb_ref[...]
    _gla_branch_rows(of_ref, ob_ref, gb_ref, nw, 0, yb_scr.at[0])
    y = None
    for s in range(n_sub + 1):
        y_prev = y
        if s < n_sub:
            y = jnp.dot(yb_scr[s % 2], w_ref[0:GLA_VAL_W, :], preferred_element_type=F32)
            y += jnp.dot(ya_ref[sub(s), :], w_ref[GLA_VAL_W:, :], preferred_element_type=F32)
        if s + 1 < n_sub:
            _gla_branch_rows(of_ref, ob_ref, gb_ref, nw, (s + 1) * SUB_TILE, yb_scr.at[(s + 1) % 2])
        if s > 0:
            _residual_ln_rows(x_ref, out_ref, (s - 1) * SUB_TILE, y_prev, gts, lg, lb)


def _even_out(o, gb, ya, norm_w, w_out, layer, x, gt, ln_g, ln_b):
    bn, L, d = x.shape
    tm = min(EVEN_OUT_TILE, L)
    assert L % tm == 0 and tm % SUB_TILE == 0
    tile = lambda n: pl.BlockSpec((None, tm, n), lambda b, i: (b, i, 0))
    odir = lambda k: pl.BlockSpec((None, None, tm, GLA_VAL_W), lambda b, i: (k, b, i, 0))
    return pl.pallas_call(
        _even_out_kernel,
        grid=(bn, L // tm),
        in_specs=[odir(0), odir(1), tile(GLA_VAL_W), tile(CONV_W), _resident(norm_w.shape),
                  _layer_resident(w_out.shape, layer), tile(d),
                  pl.BlockSpec((None, 1, d), lambda b, i: (b, 0, 0)),
                  _resident(ln_g.shape), _resident(ln_b.shape)],
        out_specs=tile(d),
        out_shape=jax.ShapeDtypeStruct((bn, L, d), F32),
        scratch_shapes=[pltpu.VMEM((2, SUB_TILE, GLA_VAL_W), BF16)],
        compiler_params=_params("parallel", "parallel"),
        name="even_out",
    )(o, o, gb, ya, norm_w, w_out, x, gt, ln_g, ln_b)


def _out_ln_kernel(m_ref, w_ref, x_ref, gt_ref, lg_ref, lb_ref, out_ref):
    n_sub = x_ref.shape[0] // SUB_TILE
    sub = lambda s: pl.ds(s * SUB_TILE, SUB_TILE)
    gts = gt_ref[...] * (1.0 / ALPHA)
    lg, lb = lg_ref[...], lb_ref[...]
    y = None
    for s in range(n_sub + 1):
        y_prev = y
        if s < n_sub:
            y = jnp.dot(m_ref[sub(s), :], w_ref[...], preferred_element_type=F32)
        if s > 0:
            _residual_ln_rows(x_ref, out_ref, (s - 1) * SUB_TILE, y_prev, gts, lg, lb)


def _out_ln(m, w_out, layer, x, gt, ln_g, ln_b):
    bn, L, d = x.shape
    tm = min(OUT_LN_TILE, L)
    assert L % tm == 0 and tm % SUB_TILE == 0
    tile = lambda n: pl.BlockSpec((None, tm, n), lambda b, i: (b, i, 0))
    return pl.pallas_call(
        _out_ln_kernel,
        grid=(bn, L // tm),
        in_specs=[tile(m.shape[-1]), _layer_resident(w_out.shape, layer), tile(d),
                  pl.BlockSpec((None, 1, d), lambda b, i: (b, 0, 0)),
                  _resident(ln_g.shape), _resident(ln_b.shape)],
        out_specs=tile(d),
        out_shape=jax.ShapeDtypeStruct((bn, L, d), F32),
        compiler_params=_params("parallel", "parallel"),
        name="out_ln",
    )(m, w_out, x, gt, ln_g, ln_b)


def _odd_mix_kernel(xp_ref, xc_ref, xn_ref, sh_ref, sc_ref, win_ref, wpool_ref, ps_ref, m_ref,
                    *, row_stride, n_rows):
    i = pl.program_id(1)
    tm = xc_ref.shape[0]
    halo = POOL_HALO_ROWS * row_stride
    xc = xc_ref[...]
    scale1 = 1.0 + sc_ref[...]
    u = (xc * scale1 + sh_ref[...]).astype(BF16)
    sgate = _silu(jnp.dot(u, win_ref[:, POOL_W:], preferred_element_type=F32)).astype(BF16)

    before = jnp.where(i > 0, xp_ref[tm - halo:, :], 0.0)
    after = jnp.where(i < pl.num_programs(1) - 1, xn_ref[:halo, :], 0.0)
    xe = jnp.concatenate([before, xc, after], axis=0)
    grid_row = (i * tm + lax.broadcasted_iota(jnp.int32, (tm, 1), 0)) // row_stride

    def window_diff(s, w):
        half = w // 2
        s = s[:s.shape[0] - half * row_stride, :] + s[half * row_stride:, :]
        start = halo - half * row_stride
        cnt = jnp.minimum(grid_row + half, n_rows) - jnp.maximum(grid_row - half, 0)
        mean = s[start:start + tm, :] / cnt.astype(F32)
        return s, ((mean - xc) * scale1).astype(BF16)

    s, diff = window_diff(xe, POOL_WINDOWS[0])
    for g in range(len(POOL_WINDOWS)):
        cur = diff
        if g + 1 < len(POOL_WINDOWS):
            s, diff = window_diff(s, POOL_WINDOWS[g + 1])
        cols = slice(g * POOL_GROUP, (g + 1) * POOL_GROUP)
        z = jnp.dot(cur, win_ref[:, cols], preferred_element_type=F32)
        z = jnp.dot(z.astype(BF16), wpool_ref[g], preferred_element_type=F32)
        m_ref[:, cols] = (z * ps_ref[:, cols] * sgate[:, cols].astype(F32)).astype(BF16)


def _odd_mix(x, sh, sc, w_in, w_pool, layer, pool_scale, row_stride):
    bn, L, d = x.shape
    n_rows = L // row_stride
    tm = min(POOL_TILE, L)
    nt = L // tm
    assert L % tm == 0 and tm % row_stride == 0 and POOL_HALO_ROWS * row_stride <= tm
    vec = pl.BlockSpec((None, 1, d), lambda b, i: (b, 0, 0))
    return pl.pallas_call(
        functools.partial(_odd_mix_kernel, row_stride=row_stride, n_rows=n_rows),
        grid=(bn, nt),
        in_specs=[pl.BlockSpec((None, tm, d), lambda b, i: (b, jnp.maximum(i - 1, 0), 0)),
                  pl.BlockSpec((None, tm, d), lambda b, i: (b, i, 0)),
                  pl.BlockSpec((None, tm, d), lambda b, i: (b, jnp.minimum(i + 1, nt - 1), 0)),
                  vec, vec, _layer_resident(w_in.shape, layer), _layer_resident(w_pool.shape, layer),
                  _resident(pool_scale.shape)],
        out_specs=pl.BlockSpec((None, tm, POOL_W), lambda b, i: (b, i, 0)),
        out_shape=jax.ShapeDtypeStruct((bn, L, POOL_W), BF16),
        compiler_params=_params("parallel", "parallel"),
        name="odd_mix",
    )(x, x, x, sh, sc, w_in, w_pool, pool_scale)


def _hi_lo(w):
    hi = w.astype(BF16)
    return hi, (w - hi.astype(F32)).astype(BF16)


def _even_w_prep_kernel(wt_ref, rank_ref, main_ref, r_ref):
    main_ref[...] = wt_ref[0].T.astype(BF16)
    rank = rank_ref[0].T[:, LANES - RANK_COLS:].astype(BF16)
    pad = jnp.zeros((rank.shape[0], LANES - 3 * RANK_COLS), BF16)
    r_ref[...] = jnp.concatenate([rank, rank, rank, pad], axis=1)


def _even_w_prep(w_in_e):
    ne, d, n = w_in_e.shape
    tn = 512
    assert STATE_W % tn == 0 and (n - RANK_COLS) % tn == 0
    wt = jnp.transpose(w_in_e, (0, 2, 1))

    def src_row(j, b):
        row = b * tn + jnp.where(b * tn >= STATE_W, RANK_COLS, 0)
        return (j, pl.multiple_of(row, RANK_COLS), 0)

    return pl.pallas_call(
        _even_w_prep_kernel,
        grid=(ne, (n - RANK_COLS) // tn),
        in_specs=[pl.BlockSpec((pl.Element(1), pl.Element(tn), pl.Element(d)), src_row),
                  pl.BlockSpec((pl.Element(1), pl.Element(LANES), pl.Element(d)),
                               lambda j, b: (j, STATE_W + RANK_COLS - LANES, 0))],
        out_specs=[pl.BlockSpec((None, d, tn), lambda j, b: (j, 0, b)),
                   pl.BlockSpec((None, d, LANES), lambda j, b: (j, 0, 0))],
        out_shape=[jax.ShapeDtypeStruct((ne, d, n - RANK_COLS), BF16),
                   jax.ShapeDtypeStruct((ne, d, LANES), BF16)],
        compiler_params=_params("parallel", "arbitrary"),
        name="even_w_prep",
    )(wt, wt)


def _gate_weights(w_gf, b_gf, w_gb, b_gb):
    zr = jnp.zeros((GLA_GATE_RANK, GLA_KEY_W), BF16)
    tail = jnp.zeros((LANES - 3 * RANK_COLS, GLA_KEY_W), BF16)
    f_hi, f_lo = _hi_lo(w_gf * LOG2_E)
    b_hi, b_lo = _hi_lo(w_gb * LOG2_E)
    wg3 = jnp.stack([jnp.concatenate([f_hi, zr, f_hi, zr, f_lo, zr, tail], axis=0),
                     jnp.concatenate([zr, b_hi, zr, b_hi, zr, b_lo, tail], axis=0)])
    bg = jnp.stack([b_gf, b_gb])[:, None, :] * LOG2_E
    return wg3, bg


def kernel(x, c, ctx, c_ctx, w_ada, b_ada, ln_g, ln_b, w_in_e, w_gate_f, b_gate_f, w_gate_b,
           b_gate_b, gla_norm_w, conv_w, w_out_e, w_in_o, w_pool, pool_scale, w_out_o):
    bn, L, d = x.shape
    lc = ctx.shape[1]
    cond = jnp.concatenate([c, c_ctx[None, :], jnp.zeros((COND_ROWS - bn - 1, d), F32)], axis=0)
    mod = _modulation(cond, w_ada, b_ada)
    s_zero = jnp.zeros((2, bn, GLA_HEADS, GLA_DV, GLA_DK), F32)
    w_main, w_r = _even_w_prep(w_in_e)
    w_out_e, w_in_o, w_out_o, w_pool = (
        w.astype(BF16) for w in (w_out_e, w_in_o, w_out_o, w_pool))
    ctx_s = ctx
    for i in range(DEPTH):
        j = i // 2
        ctx_needed = any(l % 2 == 0 for l in range(i + 1, DEPTH))
        sh, sc, gt = (mod[i, :bn, n * d:(n + 1) * d][:, None, :] for n in range(3))
        sh_c, sc_c, gt_c = (jnp.broadcast_to(mod[i, bn, n * d:(n + 1) * d], (bn, 1, d))
                            for n in range(3))
        lg, lb = ln_g[i][None, :], ln_b[i][None, :]
        if i % 2 == 0:
            wg3, bg = _gate_weights(w_gate_f[j], b_gate_f[j], w_gate_b[j], b_gate_b[j])
            norm_w = gla_norm_w[j][None, :]
            kvq_c, gb_c, ya_c, r_c = _even_in(ctx_s, sh_c, sc_c, w_main, w_r, j, conv_w[j], lc)
            o_c, s_ctx = _gla(kvq_c, r_c, wg3, bg, s_zero)
            if ctx_needed:
                ctx_s = _even_out(o_c, gb_c, ya_c, norm_w, w_out_e, j, ctx_s, gt_c, lg, lb)
            kvq, gb, ya, r = _even_in(x, sh, sc, w_main, w_r, j, conv_w[j], GRID_W)
            o, _ = _gla(kvq, r, wg3, bg, s_ctx)
            x = _even_out(o, gb, ya, norm_w, w_out_e, j, x, gt, lg, lb)
        else:
            ps = pool_scale[j][None, :]
            if ctx_needed:
                m_c = _odd_mix(ctx_s, sh_c, sc_c, w_in_o, w_pool, j, ps, 1)
                ctx_s = _out_ln(m_c, w_out_o, j, ctx_s, gt_c, lg, lb)
            m = _odd_mix(x, sh, sc, w_in_o, w_pool, j, ps, GRID_W)
            x = _out_ln(m, w_out_o, j, x, gt, lg, lb)
    return x
```

```python
import functools

import jax
import jax.numpy as jnp
from jax import lax
from jax.experimental import pallas as pl
from jax.experimental.pallas import tpu as pltpu

D_MODEL = 2048
DEPTH = 4
GRID_W = 64

GLA_HEADS = 4
GLA_DK = 256
GLA_DV = 256
GLA_KEY_W = GLA_HEADS * GLA_DK
GLA_VAL_W = GLA_HEADS * GLA_DV
GLA_GATE_RANK = 16
GLA_TAU = 16.0
CONV_W = D_MODEL // 2
POOL_W = D_MODEL
POOL_WINDOWS = (2, 4, 8, 16)
POOL_GROUP = POOL_W // len(POOL_WINDOWS)
POOL_HALO_ROWS = max(POOL_WINDOWS) // 2
LN_EPS = 1e-5
ALPHA = (2.0 * DEPTH) ** 0.25

LANES = 128
COND_ROWS = 8
VMEM_LIMIT = 60000 * 1024
GLA_CHUNK = 256
GLA_BLOCK_CHUNKS = 8
GLA_LOOKAHEAD = 2
ROW_TILE = 256
SUB_TILE = 256
EVEN_OUT_TILE = 512
OUT_LN_TILE = 1024
LN_ROWS = 8
NORM_ROWS = 16
POOL_TILE = 512
RANK_COLS = 2 * GLA_GATE_RANK
STATE_W = GLA_KEY_W + GLA_VAL_W
LOG2_E = 1.4426950408889634

F32 = jnp.float32
BF16 = jnp.bfloat16

_NT = (((1,), (1,)), ((), ()))
_TN = (((0,), (0,)), ((), ()))


def _silu(t):
    return t / (1.0 + jnp.exp(-t))


def _ordering_zero(t):
    bits = lax.bitcast_convert_type(t, jnp.uint32)
    return lax.bitcast_convert_type((bits >> 16) >> 16, F32)


def _resident(shape):
    return pl.BlockSpec(shape, lambda *_: (0,) * len(shape), pipeline_mode=pl.Buffered(1))


def _layer_resident(stacked_shape, layer):
    nd = len(stacked_shape)
    return pl.BlockSpec((None,) + tuple(stacked_shape[1:]), lambda *_: (layer,) + (0,) * (nd - 1),
                        pipeline_mode=pl.Buffered(1))


def _params(*sem):
    return pltpu.CompilerParams(dimension_semantics=sem, vmem_limit_bytes=VMEM_LIMIT)


def _modulation_kernel(c_ref, w_ref, b_ref, o_ref):
    s = _silu(c_ref[...]).astype(BF16)
    o_ref[...] = jnp.dot(s, w_ref[...].astype(BF16), preferred_element_type=F32) + b_ref[...]


def _modulation(cond, w_ada, b_ada):
    depth, d, n = w_ada.shape
    tn = 768
    return pl.pallas_call(
        _modulation_kernel,
        grid=(depth, n // tn),
        in_specs=[
            pl.BlockSpec((COND_ROWS, d), lambda i, j: (0, 0)),
            pl.BlockSpec((None, d, tn), lambda i, j: (i, 0, j)),
            pl.BlockSpec((None, 1, tn), lambda i, j: (i, 0, j)),
        ],
        out_specs=pl.BlockSpec((None, COND_ROWS, tn), lambda i, j: (i, 0, j)),
        out_shape=jax.ShapeDtypeStruct((depth, COND_ROWS, n), F32),
        compiler_params=_params("parallel", "parallel"),
        name="modulation",
    )(cond, w_ada, b_ada.reshape(depth, 1, n))


def _even_in_kernel(x_ref, sh_ref, sc_ref, w_ref, wr_ref, cw_ref,
                    kvq_ref, gb_ref, ya_ref, r_ref, *, row_w):
    tm = x_ref.shape[0]
    u = (x_ref[...] * (1.0 + sc_ref[...]) + sh_ref[...]).astype(BF16)

    def proj(col):
        return jnp.dot(u, w_ref[:, col:col + GLA_KEY_W], preferred_element_type=F32)

    kvq_ref[:, 0:1024] = proj(0).astype(BF16)
    kvq_ref[:, 1024:2048] = proj(1024).astype(BF16)
    kvq_ref[:, 2048:3072] = (proj(2048) * GLA_DK ** -0.5).astype(BF16)
    gb_ref[...] = _silu(proj(3072)).astype(BF16)

    r = jnp.dot(u, wr_ref[...], preferred_element_type=F32)
    r_hi = r.astype(BF16)
    r_lo = (r - r_hi.astype(F32)).astype(BF16)
    lane = lax.broadcasted_iota(jnp.int32, r.shape, 1)
    r_ref[...] = jnp.where((lane >= RANK_COLS) & (lane < 2 * RANK_COLS), r_lo, r_hi)

    p = proj(5120) * proj(6144)
    t = lax.broadcasted_iota(jnp.int32, (tm, 1), 0) % row_w
    prev = jnp.where(t == 0, 0.0, pltpu.roll(p, 1, axis=0))
    nxt = jnp.where(t == row_w - 1, 0.0, pltpu.roll(p, tm - 1, axis=0))
    conv = cw_ref[0:1, :] * prev + cw_ref[1:2, :] * p + cw_ref[2:3, :] * nxt
    ya_ref[...] = (proj(4096) * conv * _silu(proj(7168))).astype(BF16)


def _even_in(x, sh, sc, w_main, w_r, layer, conv_w, row_w):
    bn, L, d = x.shape
    tm = min(ROW_TILE, L)
    assert L % tm == 0 and tm % row_w == 0
    tile = lambda n: pl.BlockSpec((None, tm, n), lambda b, i: (b, i, 0))
    vec = pl.BlockSpec((None, 1, d), lambda b, i: (b, 0, 0))
    return pl.pallas_call(
        functools.partial(_even_in_kernel, row_w=row_w),
        grid=(bn, L // tm),
        in_specs=[tile(d), vec, vec, _layer_resident(w_main.shape, layer),
                  _layer_resident(w_r.shape, layer),
                  _resident(conv_w.shape)],
        out_specs=[tile(3 * GLA_KEY_W), tile(GLA_VAL_W), tile(CONV_W), tile(LANES)],
        out_shape=[jax.ShapeDtypeStruct((bn, L, 3 * GLA_KEY_W), BF16),
                   jax.ShapeDtypeStruct((bn, L, GLA_VAL_W), BF16),
                   jax.ShapeDtypeStruct((bn, L, CONV_W), BF16),
                   jax.ShapeDtypeStruct((bn, L, LANES), BF16)],
        compiler_params=_params("parallel", "parallel"),
        name="even_in",
    )(x, sh, sc, w_main, w_r, conv_w)


def _gla_prologue(q, k, r, wg, bg, tri2, fwd):
    chunk = q.shape[0]
    z2 = jnp.dot(r, wg, preferred_element_type=F32) + bg
    ls2 = jnp.minimum(z2, 0.0) - jnp.log2(1.0 + jnp.exp2(-jnp.abs(z2)))
    ls_hi = ls2.astype(BF16)
    ls_lo = (ls2 - ls_hi.astype(F32)).astype(BF16)
    g = jnp.dot(tri2, jnp.concatenate([ls_hi, ls_lo], axis=0), preferred_element_type=F32)
    p = g[chunk // 2:chunk // 2 + 1, :]
    g_tot = jnp.where(fwd, g[chunk - 1:chunk, :], g[0:1, :])
    qg = q * jnp.exp2(g - p).astype(BF16)
    kg = k * jnp.exp2(p - g).astype(BF16)
    kd = kg * jnp.exp2(g_tot - p).astype(BF16)
    return qg, kg, kd, jnp.exp2(p), jnp.exp2(g_tot)


def _gla_heads(factors, v_ref, o_ref, rows, tri, st_scr):
    qg_all, kg_all, kd_all, e_p, e_tot = factors
    first = None
    for h in range(GLA_HEADS):
        hs = slice(h * GLA_DK, (h + 1) * GLA_DK)
        qg = qg_all[:, hs]
        v = v_ref[rows, hs]
        a = lax.dot_general(qg, kg_all[:, hs], _NT, preferred_element_type=F32)
        if first is None:
            first = a[0:COND_ROWS, 0:LANES]
        a = jnp.where(tri, a, 0.0).astype(BF16)
        st = st_scr[h]
        o = jnp.dot(a, v, preferred_element_type=F32)
        o += lax.dot_general(qg, (st * e_p[:, hs]).astype(BF16), _NT, preferred_element_type=F32)
        o_ref[rows, hs] = o
        st_scr[h] = st * e_tot[:, hs] + lax.dot_general(v, kd_all[:, hs], _TN,
                                                        preferred_element_type=F32)
    return first


def _gla_kernel(k_ref, v_ref, q_ref, r_ref, wg_ref, bg_ref, s0_ref, o_ref, sfin_ref, st_scr,
                *, chunk):
    d = pl.program_id(0)
    c = pl.program_id(2)
    n_sub = k_ref.shape[0] // chunk

    @pl.when(c == 0)
    def _():
        st_scr[...] = s0_ref[...]

    row = lax.broadcasted_iota(jnp.int32, (chunk, chunk), 0)
    col = lax.broadcasted_iota(jnp.int32, (chunk, chunk), 1)
    fwd = d == 0
    tri = jnp.where(fwd, row - col, col - row) >= 0
    tri_b = jnp.where(tri, 1.0 / GLA_TAU, 0.0).astype(BF16)
    tri2 = jnp.concatenate([tri_b, tri_b], axis=1)

    def rows_of(s):
        pos = s + d * (n_sub - 1 - 2 * s)
        return pl.ds(pl.multiple_of(pos * chunk, chunk), chunk)

    def prologue(s, bg):
        rows = rows_of(s)
        return _gla_prologue(q_ref[rows, :], k_ref[rows, :], r_ref[rows, :], wg_ref[...], bg,
                             tri2, fwd)

    bg = bg_ref[...]
    factors = [prologue(s, bg) for s in range(min(GLA_LOOKAHEAD, n_sub))]
    for s in range(n_sub):
        first = _gla_heads(factors[s], v_ref, o_ref, rows_of(s), tri, st_scr)
        if s + GLA_LOOKAHEAD < n_sub:
            zero = _ordering_zero(first)[0:1, :]
            factors.append(prologue(s + GLA_LOOKAHEAD,
                                    bg + jnp.tile(zero, (1, GLA_KEY_W // LANES))))

    @pl.when(c == pl.num_programs(2) - 1)
    def _():
        sfin_ref[...] = st_scr[...]


def _gla(kvq, r3, wg3, bg, s0):
    bn, L, _ = kvq.shape
    chunk = min(GLA_CHUNK, L)
    blk = min(GLA_BLOCK_CHUNKS * chunk, L)
    nc = L // blk
    assert L % blk == 0 and blk % chunk == 0

    def cidx(d, c):
        return c + d * (nc - 1 - 2 * c)

    def part(n):
        return pl.BlockSpec((None, blk, GLA_KEY_W), lambda d, b, c: (b, cidx(d, c), n))

    state = pl.BlockSpec((None, None, GLA_HEADS, GLA_DV, GLA_DK), lambda d, b, c: (d, b, 0, 0, 0))
    return pl.pallas_call(
        functools.partial(_gla_kernel, chunk=chunk),
        grid=(2, bn, nc),
        in_specs=[part(0), part(1), part(2),
                  pl.BlockSpec((None, blk, LANES), lambda d, b, c: (b, cidx(d, c), 0)),
                  pl.BlockSpec((None, LANES, GLA_KEY_W), lambda d, b, c: (d, 0, 0)),
                  pl.BlockSpec((None, 1, GLA_KEY_W), lambda d, b, c: (d, 0, 0)),
                  state],
        out_specs=[pl.BlockSpec((None, None, blk, GLA_VAL_W), lambda d, b, c: (d, b, cidx(d, c), 0)),
                   state],
        out_shape=[jax.ShapeDtypeStruct((2, bn, L, GLA_VAL_W), F32),
                   jax.ShapeDtypeStruct((2, bn, GLA_HEADS, GLA_DV, GLA_DK), F32)],
        scratch_shapes=[pltpu.VMEM((GLA_HEADS, GLA_DV, GLA_DK), F32)],
        compiler_params=_params("parallel", "parallel", "arbitrary"),
        name="gla",
    )(kvq, kvq, kvq, r3, wg3, bg, s0)


def _residual_ln_rows(x_ref, out_ref, row0, y, gts, g, b):
    for r in range(0, y.shape[0], LN_ROWS):
        rows = pl.ds(row0 + r, LN_ROWS)
        w = x_ref[rows, :] + gts * y[r:r + LN_ROWS, :]
        mu = jnp.mean(w, axis=-1, keepdims=True)
        xc = w - mu
        var = jnp.mean(xc * xc, axis=-1, keepdims=True)
        out_ref[rows, :] = xc * lax.rsqrt(var + LN_EPS / ALPHA ** 2) * g + b


def _gla_branch_rows(of_ref, ob_ref, gb_ref, nw, row0, yb_ref):
    for r in range(0, SUB_TILE, NORM_ROWS):
        rows = pl.ds(row0 + r, NORM_ROWS)
        o = of_ref[rows, :] + ob_ref[rows, :]
        heads = []
        for h in range(GLA_HEADS):
            oh = o[:, h * GLA_DV:(h + 1) * GLA_DV]
            ms = jnp.mean(oh * oh, axis=-1, keepdims=True)
            heads.append(oh * lax.rsqrt(ms + LN_EPS) * nw)
        yb = jnp.concatenate(heads, axis=-1) * gb_ref[rows, :].astype(F32)
        yb_ref[r:r + NORM_ROWS, :] = yb.astype(BF16)


def _even_out_kernel(of_ref, ob_ref, gb_ref, ya_ref, nw_ref, w_ref, x_ref, gt_ref, lg_ref, lb_ref,
                     out_ref, yb_scr):
    n_sub = x_ref.shape[0] // SUB_TILE
    sub = lambda s: pl.ds(s * SUB_TILE, SUB_TILE)
    gts = gt_ref[...] * (1.0 / ALPHA)
    nw, lg, lb = nw_ref[...], lg_ref[...], lb_ref[...]
    _gla_branch_rows(of_ref, ob_ref, gb_ref, nw, 0, yb_scr.at[0])
    y = None
    for s in range(n_sub + 1):
        y_prev = y
        if s < n_sub:
            y = jnp.dot(yb_scr[s % 2], w_ref[0:GLA_VAL_W, :], preferred_element_type=F32)
            y += jnp.dot(ya_ref[sub(s), :], w_ref[GLA_VAL_W:, :], preferred_element_type=F32)
        if s + 1 < n_sub:
            _gla_branch_rows(of_ref, ob_ref, gb_ref, nw, (s + 1) * SUB_TILE, yb_scr.at[(s + 1) % 2])
        if s > 0:
            _residual_ln_rows(x_ref, out_ref, (s - 1) * SUB_TILE, y_prev, gts, lg, lb)


def _even_out(o, gb, ya, norm_w, w_out, layer, x, gt, ln_g, ln_b):
    bn, L, d = x.shape
    tm = min(EVEN_OUT_TILE, L)
    assert L % tm == 0 and tm % SUB_TILE == 0
    tile = lambda n: pl.BlockSpec((None, tm, n), lambda b, i: (b, i, 0))
    odir = lambda k: pl.BlockSpec((None, None, tm, GLA_VAL_W), lambda b, i: (k, b, i, 0))
    return pl.pallas_call(
        _even_out_kernel,
        grid=(bn, L // tm),
        in_specs=[odir(0), odir(1), tile(GLA_VAL_W), tile(CONV_W), _resident(norm_w.shape),
                  _layer_resident(w_out.shape, layer), tile(d),
                  pl.BlockSpec((None, 1, d), lambda b, i: (b, 0, 0)),
                  _resident(ln_g.shape), _resident(ln_b.shape)],
        out_specs=tile(d),
        out_shape=jax.ShapeDtypeStruct((bn, L, d), F32),
        scratch_shapes=[pltpu.VMEM((2, SUB_TILE, GLA_VAL_W), BF16)],
        compiler_params=_params("parallel", "parallel"),
        name="even_out",
    )(o, o, gb, ya, norm_w, w_out, x, gt, ln_g, ln_b)


def _out_ln_kernel(m_ref, w_ref, x_ref, gt_ref, lg_ref, lb_ref, out_ref):
    n_sub = x_ref.shape[0] // SUB_TILE
    sub = lambda s: pl.ds(s * SUB_TILE, SUB_TILE)
    gts = gt_ref[...] * (1.0 / ALPHA)
    lg, lb = lg_ref[...], lb_ref[...]
    y = None
    for s in range(n_sub + 1):
        y_prev = y
        if s < n_sub:
            y = jnp.dot(m_ref[sub(s), :], w_ref[...], preferred_element_type=F32)
        if s > 0:
            _residual_ln_rows(x_ref, out_ref, (s - 1) * SUB_TILE, y_prev, gts, lg, lb)


def _out_ln(m, w_out, layer, x, gt, ln_g, ln_b):
    bn, L, d = x.shape
    tm = min(OUT_LN_TILE, L)
    assert L % tm == 0 and tm % SUB_TILE == 0
    tile = lambda n: pl.BlockSpec((None, tm, n), lambda b, i: (b, i, 0))
    return pl.pallas_call(
        _out_ln_kernel,
        grid=(bn, L // tm),
        in_specs=[tile(m.shape[-1]), _layer_resident(w_out.shape, layer), tile(d),
                  pl.BlockSpec((None, 1, d), lambda b, i: (b, 0, 0)),
                  _resident(ln_g.shape), _resident(ln_b.shape)],
        out_specs=tile(d),
        out_shape=jax.ShapeDtypeStruct((bn, L, d), F32),
        compiler_params=_params("parallel", "parallel"),
        name="out_ln",
    )(m, w_out, x, gt, ln_g, ln_b)


def _odd_mix_kernel(xp_ref, xc_ref, xn_ref, sh_ref, sc_ref, win_ref, wpool_ref, ps_ref, m_ref,
                    *, row_stride, n_rows):
    i = pl.program_id(1)
    tm = xc_ref.shape[0]
    halo = POOL_HALO_ROWS * row_stride
    xc = xc_ref[...]
    scale1 = 1.0 + sc_ref[...]
    u = (xc * scale1 + sh_ref[...]).astype(BF16)
    sgate = _silu(jnp.dot(u, win_ref[:, POOL_W:], preferred_element_type=F32)).astype(BF16)

    before = jnp.where(i > 0, xp_ref[tm - halo:, :], 0.0)
    after = jnp.where(i < pl.num_programs(1) - 1, xn_ref[:halo, :], 0.0)
    xe = jnp.concatenate([before, xc, after], axis=0)
    grid_row = (i * tm + lax.broadcasted_iota(jnp.int32, (tm, 1), 0)) // row_stride

    def window_diff(s, w):
        half = w // 2
        s = s[:s.shape[0] - half * row_stride, :] + s[half * row_stride:, :]
        start = halo - half * row_stride
        cnt = jnp.minimum(grid_row + half, n_rows) - jnp.maximum(grid_row - half, 0)
        mean = s[start:start + tm, :] / cnt.astype(F32)
        return s, ((mean - xc) * scale1).astype(BF16)

    s, diff = window_diff(xe, POOL_WINDOWS[0])
    for g in range(len(POOL_WINDOWS)):
        cur = diff
        if g + 1 < len(POOL_WINDOWS):
            s, diff = window_diff(s, POOL_WINDOWS[g + 1])
        cols = slice(g * POOL_GROUP, (g + 1) * POOL_GROUP)
        z = jnp.dot(cur, win_ref[:, cols], preferred_element_type=F32)
        z = jnp.dot(z.astype(BF16), wpool_ref[g], preferred_element_type=F32)
        m_ref[:, cols] = (z * ps_ref[:, cols] * sgate[:, cols].astype(F32)).astype(BF16)


def _odd_mix(x, sh, sc, w_in, w_pool, layer, pool_scale, row_stride):
    bn, L, d = x.shape
    n_rows = L // row_stride
    tm = min(POOL_TILE, L)
    nt = L // tm
    assert L % tm == 0 and tm % row_stride == 0 and POOL_HALO_ROWS * row_stride <= tm
    vec = pl.BlockSpec((None, 1, d), lambda b, i: (b, 0, 0))
    return pl.pallas_call(
        functools.partial(_odd_mix_kernel, row_stride=row_stride, n_rows=n_rows),
        grid=(bn, nt),
        in_specs=[pl.BlockSpec((None, tm, d), lambda b, i: (b, jnp.maximum(i - 1, 0), 0)),
                  pl.BlockSpec((None, tm, d), lambda b, i: (b, i, 0)),
                  pl.BlockSpec((None, tm, d), lambda b, i: (b, jnp.minimum(i + 1, nt - 1), 0)),
                  vec, vec, _layer_resident(w_in.shape, layer), _layer_resident(w_pool.shape, layer),
                  _resident(pool_scale.shape)],
        out_specs=pl.BlockSpec((None, tm, POOL_W), lambda b, i: (b, i, 0)),
        out_shape=jax.ShapeDtypeStruct((bn, L, POOL_W), BF16),
        compiler_params=_params("parallel", "parallel"),
        name="odd_mix",
    )(x, x, x, sh, sc, w_in, w_pool, pool_scale)


def _hi_lo(w):
    hi = w.astype(BF16)
    return hi, (w - hi.astype(F32)).astype(BF16)


def _even_w_prep_kernel(wt_ref, rank_ref, main_ref, r_ref):
    main_ref[...] = wt_ref[0].T.astype(BF16)
    rank = rank_ref[0].T[:, LANES - RANK_COLS:].astype(BF16)
    pad = jnp.zeros((rank.shape[0], LANES - 3 * RANK_COLS), BF16)
    r_ref[...] = jnp.concatenate([rank, rank, rank, pad], axis=1)


def _even_w_prep(w_in_e):
    ne, d, n = w_in_e.shape
    tn = 512
    assert STATE_W % tn == 0 and (n - RANK_COLS) % tn == 0
    wt = jnp.transpose(w_in_e, (0, 2, 1))

    def src_row(j, b):
        row = b * tn + jnp.where(b * tn >= STATE_W, RANK_COLS, 0)
        return (j, pl.multiple_of(row, RANK_COLS), 0)

    return pl.pallas_call(
        _even_w_prep_kernel,
        grid=(ne, (n - RANK_COLS) // tn),
        in_specs=[pl.BlockSpec((pl.Element(1), pl.Element(tn), pl.Element(d)), src_row),
                  pl.BlockSpec((pl.Element(1), pl.Element(LANES), pl.Element(d)),
                               lambda j, b: (j, STATE_W + RANK_COLS - LANES, 0))],
        out_specs=[pl.BlockSpec((None, d, tn), lambda j, b: (j, 0, b)),
                   pl.BlockSpec((None, d, LANES), lambda j, b: (j, 0, 0))],
        out_shape=[jax.ShapeDtypeStruct((ne, d, n - RANK_COLS), BF16),
                   jax.ShapeDtypeStruct((ne, d, LANES), BF16)],
        compiler_params=_params("parallel", "arbitrary"),
        name="even_w_prep",
    )(wt, wt)


def _gate_weights(w_gf, b_gf, w_gb, b_gb):
    zr = jnp.zeros((GLA_GATE_RANK, GLA_KEY_W), BF16)
    tail = jnp.zeros((LANES - 3 * RANK_COLS, GLA_KEY_W), BF16)
    f_hi, f_lo = _hi_lo(w_gf * LOG2_E)
    b_hi, b_lo = _hi_lo(w_gb * LOG2_E)
    wg3 = jnp.stack([jnp.concatenate([f_hi, zr, f_hi, zr, f_lo, zr, tail], axis=0),
                     jnp.concatenate([zr, b_hi, zr, b_hi, zr, b_lo, tail], axis=0)])
    bg = jnp.stack([b_gf, b_gb])[:, None, :] * LOG2_E
    return wg3, bg


def kernel(x, c, ctx, c_ctx, w_ada, b_ada, ln_g, ln_b, w_in_e, w_gate_f, b_gate_f, w_gate_b,
           b_gate_b, gla_norm_w, conv_w, w_out_e, w_in_o, w_pool, pool_scale, w_out_o):
    bn, L, d = x.shape
    lc = ctx.shape[1]
    cond = jnp.concatenate([c, c_ctx[None, :], jnp.zeros((COND_ROWS - bn - 1, d), F32)], axis=0)
    mod = _modulation(cond, w_ada, b_ada)
    s_zero = jnp.zeros((2, bn, GLA_HEADS, GLA_DV, GLA_DK), F32)
    w_main, w_r = _even_w_prep(w_in_e)
    w_out_e, w_in_o, w_out_o, w_pool = (
        w.astype(BF16) for w in (w_out_e, w_in_o, w_out_o, w_pool))
    ctx_s = ctx
    for i in range(DEPTH):
        j = i // 2
        ctx_needed = any(l % 2 == 0 for l in range(i + 1, DEPTH))
        sh, sc, gt = (mod[i, :bn, n * d:(n + 1) * d][:, None, :] for n in range(3))
        sh_c, sc_c, gt_c = (jnp.broadcast_to(mod[i, bn, n * d:(n + 1) * d], (bn, 1, d))
                            for n in range(3))
        lg, lb = ln_g[i][None, :], ln_b[i][None, :]
        if i % 2 == 0:
            wg3, bg = _gate_weights(w_gate_f[j], b_gate_f[j], w_gate_b[j], b_gate_b[j])
            norm_w = gla_norm_w[j][None, :]
            kvq_c, gb_c, ya_c, r_c = _even_in(ctx_s, sh_c, sc_c, w_main, w_r, j, conv_w[j], lc)
            o_c, s_ctx = _gla(kvq_c, r_c, wg3, bg, s_zero)
            if ctx_needed:
                ctx_s = _even_out(o_c, gb_c, ya_c, norm_w, w_out_e, j, ctx_s, gt_c, lg, lb)
            kvq, gb, ya, r = _even_in(x, sh, sc, w_main, w_r, j, conv_w[j], GRID_W)
            o, _ = _gla(kvq, r, wg3, bg, s_ctx)
            x = _even_out(o, gb, ya, norm_w, w_out_e, j, x, gt, lg, lb)
        else:
            ps = pool_scale[j][None, :]
            if ctx_needed:
                m_c = _odd_mix(ctx_s, sh_c, sc_c, w_in_o, w_pool, j, ps, 1)
                ctx_s = _out_ln(m_c, w_out_o, j, ctx_s, gt_c, lg, lb)
            m = _odd_mix(x, sh, sc, w_in_o, w_pool, j, ps, GRID_W)
            x = _out_ln(m, w_out_o, j, x, gt, lg, lb)
    return x
```

```python
import functools

import jax
import jax.numpy as jnp
from jax import lax
from jax.experimental import pallas as pl
from jax.experimental.pallas import tpu as pltpu

D_MODEL = 2048
DEPTH = 4
GRID_W = 64

GLA_HEADS = 4
GLA_DK = 256
GLA_DV = 256
GLA_KEY_W = GLA_HEADS * GLA_DK
GLA_VAL_W = GLA_HEADS * GLA_DV
GLA_GATE_RANK = 16
GLA_TAU = 16.0
CONV_W = D_MODEL // 2
POOL_W = D_MODEL
POOL_WINDOWS = (2, 4, 8, 16)
POOL_GROUP = POOL_W // len(POOL_WINDOWS)
POOL_HALO_ROWS = max(POOL_WINDOWS) // 2
LN_EPS = 1e-5
ALPHA = (2.0 * DEPTH) ** 0.25

LANES = 128
COND_ROWS = 8
VMEM_LIMIT = 60000 * 1024
GLA_CHUNK = 256
GLA_BLOCK_CHUNKS = 8
SUB_TILE = 256
EVEN_OUT_TILE = 512
OUT_LN_TILE = 1024
LN_ROWS = 8
NORM_ROWS = 16
POOL_TILE = 512
RANK_COLS = 2 * GLA_GATE_RANK
STATE_W = GLA_KEY_W + GLA_VAL_W
LOG2_E = 1.4426950408889634

F32 = jnp.float32
BF16 = jnp.bfloat16

_NT = (((1,), (1,)), ((), ()))
_TN = (((0,), (0,)), ((), ()))


def _silu(t):
    return t / (1.0 + jnp.exp(-t))


def _resident(shape):
    return pl.BlockSpec(shape, lambda *_: (0,) * len(shape), pipeline_mode=pl.Buffered(1))


def _layer_resident(stacked_shape, layer):
    nd = len(stacked_shape)
    return pl.BlockSpec((None,) + tuple(stacked_shape[1:]), lambda *_: (layer,) + (0,) * (nd - 1),
                        pipeline_mode=pl.Buffered(1))


def _params(*sem):
    return pltpu.CompilerParams(dimension_semantics=sem, vmem_limit_bytes=VMEM_LIMIT)


def _modulation_kernel(c_ref, w_ref, b_ref, o_ref):
    s = _silu(c_ref[...]).astype(BF16)
    o_ref[...] = jnp.dot(s, w_ref[...].astype(BF16), preferred_element_type=F32) + b_ref[...]


def _modulation(cond, w_ada, b_ada):
    depth, d, n = w_ada.shape
    tn = 768
    return pl.pallas_call(
        _modulation_kernel,
        grid=(depth, n // tn),
        in_specs=[
            pl.BlockSpec((COND_ROWS, d), lambda i, j: (0, 0)),
            pl.BlockSpec((None, d, tn), lambda i, j: (i, 0, j)),
            pl.BlockSpec((None, 1, tn), lambda i, j: (i, 0, j)),
        ],
        out_specs=pl.BlockSpec((None, COND_ROWS, tn), lambda i, j: (i, 0, j)),
        out_shape=jax.ShapeDtypeStruct((depth, COND_ROWS, n), F32),
        compiler_params=_params("parallel", "parallel"),
        name="modulation",
    )(cond, w_ada, b_ada.reshape(depth, 1, n))


def _gate_log2(r3, wg, bg):
    z2 = jnp.dot(r3, wg, preferred_element_type=F32) + bg
    ls2 = jnp.minimum(z2, 0.0) - jnp.log2(1.0 + jnp.exp2(-jnp.abs(z2)))
    ls_hi = ls2.astype(BF16)
    ls_lo = (ls2 - ls_hi.astype(F32)).astype(BF16)
    return jnp.concatenate([ls_hi, ls_lo], axis=0)


def _running_decay(ls, fwd):
    chunk = ls.shape[0] // 2
    row = lax.broadcasted_iota(jnp.int32, (chunk, chunk), 0)
    col = lax.broadcasted_iota(jnp.int32, (chunk, chunk), 1)
    tri_b = jnp.where((row >= col) if fwd else (row <= col), 1.0 / GLA_TAU, 0.0).astype(BF16)
    return jnp.dot(jnp.concatenate([tri_b, tri_b], axis=1), ls, preferred_element_type=F32)


def _store_decay_factors(q, k, g, fwd, fac_ref, scl_ref):
    chunk = q.shape[0]
    p = g[chunk // 2:chunk // 2 + 1, :]
    g_tot = g[chunk - 1:chunk, :] if fwd else g[0:1, :]
    kg = k * jnp.exp2(p - g).astype(BF16)
    fac_ref[:, 0:GLA_KEY_W] = q * jnp.exp2(g - p).astype(BF16)
    fac_ref[:, GLA_KEY_W:2 * GLA_KEY_W] = kg
    fac_ref[:, 2 * GLA_KEY_W:] = kg * jnp.exp2(g_tot - p).astype(BF16)
    pad = jnp.zeros((COND_ROWS - 2, GLA_KEY_W), F32)
    scl_ref[...] = jnp.concatenate([jnp.exp2(p), jnp.exp2(g_tot), pad], axis=0)


def _even_in_kernel(x_ref, sh_ref, sc_ref, w_ref, wr_ref, cw_ref, wg_ref, bg_ref,
                    v_ref, fac_ref, scl_ref, gb_ref, ya_ref, *, row_w):
    tm = x_ref.shape[0]
    u = (x_ref[...] * (1.0 + sc_ref[...]) + sh_ref[...]).astype(BF16)

    def proj(col):
        return jnp.dot(u, w_ref[:, col:col + GLA_KEY_W], preferred_element_type=F32)

    r = jnp.dot(u, wr_ref[...], preferred_element_type=F32)
    r_hi = r.astype(BF16)
    r_lo = (r - r_hi.astype(F32)).astype(BF16)
    lane = lax.broadcasted_iota(jnp.int32, r.shape, 1)
    r3 = jnp.where((lane >= RANK_COLS) & (lane < 2 * RANK_COLS), r_lo, r_hi)

    k = proj(0).astype(BF16)
    ls = [_gate_log2(r3, wg_ref[d], bg_ref[d]) for d in range(2)]
    v_ref[...] = proj(1024).astype(BF16)
    q = (proj(2048) * GLA_DK ** -0.5).astype(BF16)
    g = [_running_decay(ls[d], d == 0) for d in range(2)]
    gb_ref[...] = _silu(proj(3072)).astype(BF16)
    for d in range(2):
        _store_decay_factors(q, k, g[d], d == 0, fac_ref.at[d], scl_ref.at[d])

    p = proj(5120) * proj(6144)
    t = lax.broadcasted_iota(jnp.int32, (tm, 1), 0) % row_w
    prev = jnp.where(t == 0, 0.0, pltpu.roll(p, 1, axis=0))
    nxt = jnp.where(t == row_w - 1, 0.0, pltpu.roll(p, tm - 1, axis=0))
    conv = cw_ref[0:1, :] * prev + cw_ref[1:2, :] * p + cw_ref[2:3, :] * nxt
    ya_ref[...] = (proj(4096) * conv * _silu(proj(7168))).astype(BF16)


def _even_in(x, sh, sc, w_main, w_r, layer, conv_w, wg3, bg, row_w):
    bn, L, d = x.shape
    tm = min(GLA_CHUNK, L)
    assert L % tm == 0 and tm % row_w == 0
    tile = lambda n: pl.BlockSpec((None, tm, n), lambda b, i: (b, i, 0))
    vec = pl.BlockSpec((None, 1, d), lambda b, i: (b, 0, 0))
    return pl.pallas_call(
        functools.partial(_even_in_kernel, row_w=row_w),
        grid=(bn, L // tm),
        in_specs=[tile(d), vec, vec, _layer_resident(w_main.shape, layer),
                  _layer_resident(w_r.shape, layer),
                  _resident(conv_w.shape), _resident(wg3.shape), _resident(bg.shape)],
        out_specs=[tile(GLA_VAL_W),
                   pl.BlockSpec((2, None, tm, 3 * GLA_KEY_W), lambda b, i: (0, b, i, 0)),
                   pl.BlockSpec((2, None, None, COND_ROWS, GLA_KEY_W), lambda b, i: (0, b, i, 0, 0)),
                   tile(GLA_VAL_W), tile(CONV_W)],
        out_shape=[jax.ShapeDtypeStruct((bn, L, GLA_VAL_W), BF16),
                   jax.ShapeDtypeStruct((2, bn, L, 3 * GLA_KEY_W), BF16),
                   jax.ShapeDtypeStruct((2, bn, L // tm, COND_ROWS, GLA_KEY_W), F32),
                   jax.ShapeDtypeStruct((bn, L, GLA_VAL_W), BF16),
                   jax.ShapeDtypeStruct((bn, L, CONV_W), BF16)],
        compiler_params=_params("parallel", "parallel"),
        name="even_in",
    )(x, sh, sc, w_main, w_r, conv_w, wg3, bg)


def _gla_kernel(fac_ref, scl_ref, v_ref, s0_ref, o_ref, sfin_ref, st_scr, *, chunk):
    d = pl.program_id(0)
    c = pl.program_id(2)
    n_sub = v_ref.shape[0] // chunk

    @pl.when(c == 0)
    def _():
        st_scr[...] = s0_ref[...]

    row = lax.broadcasted_iota(jnp.int32, (chunk, chunk), 0)
    col = lax.broadcasted_iota(jnp.int32, (chunk, chunk), 1)
    tri = jnp.where(d == 0, row - col, col - row) >= 0

    for s in range(n_sub):
        pos = s + d * (n_sub - 1 - 2 * s)
        rows = pl.ds(pl.multiple_of(pos * chunk, chunk), chunk)
        e_p = scl_ref[pos, 0:1, :]
        e_tot = scl_ref[pos, 1:2, :]
        for h in range(GLA_HEADS):
            hs = slice(h * GLA_DK, (h + 1) * GLA_DK)
            qg = fac_ref[rows, hs]
            kg = fac_ref[rows, GLA_KEY_W + h * GLA_DK:GLA_KEY_W + (h + 1) * GLA_DK]
            kd = fac_ref[rows, 2 * GLA_KEY_W + h * GLA_DK:2 * GLA_KEY_W + (h + 1) * GLA_DK]
            v = v_ref[rows, hs]
            a = lax.dot_general(qg, kg, _NT, preferred_element_type=F32)
            a = jnp.where(tri, a, 0.0).astype(BF16)
            st = st_scr[h]
            o = jnp.dot(a, v, preferred_element_type=F32)
            o += lax.dot_general(qg, (st * e_p[:, hs]).astype(BF16), _NT,
                                 preferred_element_type=F32)
            o_ref[rows, hs] = o
            st_scr[h] = st * e_tot[:, hs] + lax.dot_general(v, kd, _TN,
                                                            preferred_element_type=F32)

    @pl.when(c == pl.num_programs(2) - 1)
    def _():
        sfin_ref[...] = st_scr[...]


def _gla(v, fac, scl, s0):
    bn, L, _ = v.shape
    chunk = min(GLA_CHUNK, L)
    blk = min(GLA_BLOCK_CHUNKS * chunk, L)
    nc = L // blk
    assert L % blk == 0 and blk % chunk == 0

    def cidx(d, c):
        return c + d * (nc - 1 - 2 * c)

    state = pl.BlockSpec((None, None, GLA_HEADS, GLA_DV, GLA_DK), lambda d, b, c: (d, b, 0, 0, 0))
    return pl.pallas_call(
        functools.partial(_gla_kernel, chunk=chunk),
        grid=(2, bn, nc),
        in_specs=[pl.BlockSpec((None, None, blk, 3 * GLA_KEY_W),
                               lambda d, b, c: (d, b, cidx(d, c), 0)),
                  pl.BlockSpec((None, None, blk // chunk, COND_ROWS, GLA_KEY_W),
                               lambda d, b, c: (d, b, cidx(d, c), 0, 0)),
                  pl.BlockSpec((None, blk, GLA_VAL_W), lambda d, b, c: (b, cidx(d, c), 0)),
                  state],
        out_specs=[pl.BlockSpec((None, None, blk, GLA_VAL_W), lambda d, b, c: (d, b, cidx(d, c), 0)),
                   state],
        out_shape=[jax.ShapeDtypeStruct((2, bn, L, GLA_VAL_W), F32),
                   jax.ShapeDtypeStruct((2, bn, GLA_HEADS, GLA_DV, GLA_DK), F32)],
        scratch_shapes=[pltpu.VMEM((GLA_HEADS, GLA_DV, GLA_DK), F32)],
        compiler_params=_params("parallel", "parallel", "arbitrary"),
        name="gla",
    )(fac, scl, v, s0)


def _residual_ln_rows(x_ref, out_ref, row0, y, gts, g, b):
    for r in range(0, y.shape[0], LN_ROWS):
        rows = pl.ds(row0 + r, LN_ROWS)
        w = x_ref[rows, :] + gts * y[r:r + LN_ROWS, :]
        mu = jnp.mean(w, axis=-1, keepdims=True)
        xc = w - mu
        var = jnp.mean(xc * xc, axis=-1, keepdims=True)
        out_ref[rows, :] = xc * lax.rsqrt(var + LN_EPS / ALPHA ** 2) * g + b


def _gla_branch_rows(of_ref, ob_ref, gb_ref, nw, row0, yb_ref):
    for r in range(0, SUB_TILE, NORM_ROWS):
        rows = pl.ds(row0 + r, NORM_ROWS)
        o = of_ref[rows, :] + ob_ref[rows, :]
        heads = []
        for h in range(GLA_HEADS):
            oh = o[:, h * GLA_DV:(h + 1) * GLA_DV]
            ms = jnp.mean(oh * oh, axis=-1, keepdims=True)
            heads.append(oh * lax.rsqrt(ms + LN_EPS) * nw)
        yb = jnp.concatenate(heads, axis=-1) * gb_ref[rows, :].astype(F32)
        yb_ref[r:r + NORM_ROWS, :] = yb.astype(BF16)


def _even_out_kernel(of_ref, ob_ref, gb_ref, ya_ref, nw_ref, w_ref, x_ref, gt_ref, lg_ref, lb_ref,
                     out_ref, yb_scr):
    n_sub = x_ref.shape[0] // SUB_TILE
    sub = lambda s: pl.ds(s * SUB_TILE, SUB_TILE)
    gts = gt_ref[...] * (1.0 / ALPHA)
    nw, lg, lb = nw_ref[...], lg_ref[...], lb_ref[...]
    _gla_branch_rows(of_ref, ob_ref, gb_ref, nw, 0, yb_scr.at[0])
    y = None
    for s in range(n_sub + 1):
        y_prev = y
        if s < n_sub:
            y = jnp.dot(yb_scr[s % 2], w_ref[0:GLA_VAL_W, :], preferred_element_type=F32)
            y += jnp.dot(ya_ref[sub(s), :], w_ref[GLA_VAL_W:, :], preferred_element_type=F32)
        if s + 1 < n_sub:
            _gla_branch_rows(of_ref, ob_ref, gb_ref, nw, (s + 1) * SUB_TILE, yb_scr.at[(s + 1) % 2])
        if s > 0:
            _residual_ln_rows(x_ref, out_ref, (s - 1) * SUB_TILE, y_prev, gts, lg, lb)


def _even_out(o, gb, ya, norm_w, w_out, layer, x, gt, ln_g, ln_b):
    bn, L, d = x.shape
    tm = min(EVEN_OUT_TILE, L)
    assert L % tm == 0 and tm % SUB_TILE == 0
    tile = lambda n: pl.BlockSpec((None, tm, n), lambda b, i: (b, i, 0))
    odir = lambda k: pl.BlockSpec((None, None, tm, GLA_VAL_W), lambda b, i: (k, b, i, 0))
    return pl.pallas_call(
        _even_out_kernel,
        grid=(bn, L // tm),
        in_specs=[odir(0), odir(1), tile(GLA_VAL_W), tile(CONV_W), _resident(norm_w.shape),
                  _layer_resident(w_out.shape, layer), tile(d),
                  pl.BlockSpec((None, 1, d), lambda b, i: (b, 0, 0)),
                  _resident(ln_g.shape), _resident(ln_b.shape)],
        out_specs=tile(d),
        out_shape=jax.ShapeDtypeStruct((bn, L, d), F32),
        scratch_shapes=[pltpu.VMEM((2, SUB_TILE, GLA_VAL_W), BF16)],
        compiler_params=_params("parallel", "parallel"),
        name="even_out",
    )(o, o, gb, ya, norm_w, w_out, x, gt, ln_g, ln_b)


def _out_ln_kernel(m_ref, w_ref, x_ref, gt_ref, lg_ref, lb_ref, out_ref):
    n_sub = x_ref.shape[0] // SUB_TILE
    sub = lambda s: pl.ds(s * SUB_TILE, SUB_TILE)
    gts = gt_ref[...] * (1.0 / ALPHA)
    lg, lb = lg_ref[...], lb_ref[...]
    y = None
    for s in range(n_sub + 1):
        y_prev = y
        if s < n_sub:
            y = jnp.dot(m_ref[sub(s), :], w_ref[...], preferred_element_type=F32)
        if s > 0:
            _residual_ln_rows(x_ref, out_ref, (s - 1) * SUB_TILE, y_prev, gts, lg, lb)


def _out_ln(m, w_out, layer, x, gt, ln_g, ln_b):
    bn, L, d = x.shape
    tm = min(OUT_LN_TILE, L)
    assert L % tm == 0 and tm % SUB_TILE == 0
    tile = lambda n: pl.BlockSpec((None, tm, n), lambda b, i: (b, i, 0))
    return pl.pallas_call(
        _out_ln_kernel,
        grid=(bn, L // tm),
        in_specs=[tile(m.shape[-1]), _layer_resident(w_out.shape, layer), tile(d),
                  pl.BlockSpec((None, 1, d), lambda b, i: (b, 0, 0)),
                  _resident(ln_g.shape), _resident(ln_b.shape)],
        out_specs=tile(d),
        out_shape=jax.ShapeDtypeStruct((bn, L, d), F32),
        compiler_params=_params("parallel", "parallel"),
        name="out_ln",
    )(m, w_out, x, gt, ln_g, ln_b)


def _odd_mix_kernel(xp_ref, xc_ref, xn_ref, sh_ref, sc_ref, win_ref, wpool_ref, ps_ref, m_ref,
                    *, row_stride, n_rows):
    i = pl.program_id(1)
    tm = xc_ref.shape[0]
    halo = POOL_HALO_ROWS * row_stride
    xc = xc_ref[...]
    scale1 = 1.0 + sc_ref[...]
    u = (xc * scale1 + sh_ref[...]).astype(BF16)
    sgate = _silu(jnp.dot(u, win_ref[:, POOL_W:], preferred_element_type=F32)).astype(BF16)

    before = jnp.where(i > 0, xp_ref[tm - halo:, :], 0.0)
    after = jnp.where(i < pl.num_programs(1) - 1, xn_ref[:halo, :], 0.0)
    xe = jnp.concatenate([before, xc, after], axis=0)
    grid_row = (i * tm + lax.broadcasted_iota(jnp.int32, (tm, 1), 0)) // row_stride

    def window_diff(s, w):
        half = w // 2
        s = s[:s.shape[0] - half * row_stride, :] + s[half * row_stride:, :]
        start = halo - half * row_stride
        cnt = jnp.minimum(grid_row + half, n_rows) - jnp.maximum(grid_row - half, 0)
        mean = s[start:start + tm, :] / cnt.astype(F32)
        return s, ((mean - xc) * scale1).astype(BF16)

    s, diff = window_diff(xe, POOL_WINDOWS[0])
    for g in range(len(POOL_WINDOWS)):
        cur = diff
        if g + 1 < len(POOL_WINDOWS):
            s, diff = window_diff(s, POOL_WINDOWS[g + 1])
        cols = slice(g * POOL_GROUP, (g + 1) * POOL_GROUP)
        z = jnp.dot(cur, win_ref[:, cols], preferred_element_type=F32)
        z = jnp.dot(z.astype(BF16), wpool_ref[g], preferred_element_type=F32)
        m_ref[:, cols] = (z * ps_ref[:, cols] * sgate[:, cols].astype(F32)).astype(BF16)


def _odd_mix(x, sh, sc, w_in, w_pool, layer, pool_scale, row_stride):
    bn, L, d = x.shape
    n_rows = L // row_stride
    tm = min(POOL_TILE, L)
    nt = L // tm
    assert L % tm == 0 and tm % row_stride == 0 and POOL_HALO_ROWS * row_stride <= tm
    vec = pl.BlockSpec((None, 1, d), lambda b, i: (b, 0, 0))
    return pl.pallas_call(
        functools.partial(_odd_mix_kernel, row_stride=row_stride, n_rows=n_rows),
        grid=(bn, nt),
        in_specs=[pl.BlockSpec((None, tm, d), lambda b, i: (b, jnp.maximum(i - 1, 0), 0)),
                  pl.BlockSpec((None, tm, d), lambda b, i: (b, i, 0)),
                  pl.BlockSpec((None, tm, d), lambda b, i: (b, jnp.minimum(i + 1, nt - 1), 0)),
                  vec, vec, _layer_resident(w_in.shape, layer), _layer_resident(w_pool.shape, layer),
                  _resident(pool_scale.shape)],
        out_specs=pl.BlockSpec((None, tm, POOL_W), lambda b, i: (b, i, 0)),
        out_shape=jax.ShapeDtypeStruct((bn, L, POOL_W), BF16),
        compiler_params=_params("parallel", "parallel"),
        name="odd_mix",
    )(x, x, x, sh, sc, w_in, w_pool, pool_scale)


def _hi_lo(w):
    hi = w.astype(BF16)
    return hi, (w - hi.astype(F32)).astype(BF16)


def _even_w_prep_kernel(wt_ref, rank_ref, main_ref, r_ref):
    main_ref[...] = wt_ref[0].T.astype(BF16)
    rank = rank_ref[0].T[:, LANES - RANK_COLS:].astype(BF16)
    pad = jnp.zeros((rank.shape[0], LANES - 3 * RANK_COLS), BF16)
    r_ref[...] = jnp.concatenate([rank, rank, rank, pad], axis=1)


def _even_w_prep(w_in_e):
    ne, d, n = w_in_e.shape
    tn = 512
    assert STATE_W % tn == 0 and (n - RANK_COLS) % tn == 0
    wt = jnp.transpose(w_in_e, (0, 2, 1))

    def src_row(j, b):
        row = b * tn + jnp.where(b * tn >= STATE_W, RANK_COLS, 0)
        return (j, pl.multiple_of(row, RANK_COLS), 0)

    return pl.pallas_call(
        _even_w_prep_kernel,
        grid=(ne, (n - RANK_COLS) // tn),
        in_specs=[pl.BlockSpec((pl.Element(1), pl.Element(tn), pl.Element(d)), src_row),
                  pl.BlockSpec((pl.Element(1), pl.Element(LANES), pl.Element(d)),
                               lambda j, b: (j, STATE_W + RANK_COLS - LANES, 0))],
        out_specs=[pl.BlockSpec((None, d, tn), lambda j, b: (j, 0, b)),
                   pl.BlockSpec((None, d, LANES), lambda j, b: (j, 0, 0))],
        out_shape=[jax.ShapeDtypeStruct((ne, d, n - RANK_COLS), BF16),
                   jax.ShapeDtypeStruct((ne, d, LANES), BF16)],
        compiler_params=_params("parallel", "arbitrary"),
        name="even_w_prep",
    )(wt, wt)


def _gate_weights(w_gf, b_gf, w_gb, b_gb):
    zr = jnp.zeros((GLA_GATE_RANK, GLA_KEY_W), BF16)
    tail = jnp.zeros((LANES - 3 * RANK_COLS, GLA_KEY_W), BF16)
    f_hi, f_lo = _hi_lo(w_gf * LOG2_E)
    b_hi, b_lo = _hi_lo(w_gb * LOG2_E)
    wg3 = jnp.stack([jnp.concatenate([f_hi, zr, f_hi, zr, f_lo, zr, tail], axis=0),
                     jnp.concatenate([zr, b_hi, zr, b_hi, zr, b_lo, tail], axis=0)])
    bg = jnp.stack([b_gf, b_gb])[:, None, :] * LOG2_E
    return wg3, bg


def kernel(x, c, ctx, c_ctx, w_ada, b_ada, ln_g, ln_b, w_in_e, w_gate_f, b_gate_f, w_gate_b,
           b_gate_b, gla_norm_w, conv_w, w_out_e, w_in_o, w_pool, pool_scale, w_out_o):
    bn, L, d = x.shape
    lc = ctx.shape[1]
    cond = jnp.concatenate([c, c_ctx[None, :], jnp.zeros((COND_ROWS - bn - 1, d), F32)], axis=0)
    mod = _modulation(cond, w_ada, b_ada)
    s_zero = jnp.zeros((2, bn, GLA_HEADS, GLA_DV, GLA_DK), F32)
    w_main, w_r = _even_w_prep(w_in_e)
    w_out_e, w_in_o, w_out_o, w_pool = (
        w.astype(BF16) for w in (w_out_e, w_in_o, w_out_o, w_pool))
    ctx_s = ctx
    for i in range(DEPTH):
        j = i // 2
        ctx_needed = any(l % 2 == 0 for l in range(i + 1, DEPTH))
        sh, sc, gt = (mod[i, :bn, n * d:(n + 1) * d][:, None, :] for n in range(3))
        sh_c, sc_c, gt_c = (jnp.broadcast_to(mod[i, bn, n * d:(n + 1) * d], (bn, 1, d))
                            for n in range(3))
        lg, lb = ln_g[i][None, :], ln_b[i][None, :]
        if i % 2 == 0:
            wg3, bg = _gate_weights(w_gate_f[j], b_gate_f[j], w_gate_b[j], b_gate_b[j])
            norm_w = gla_norm_w[j][None, :]
            v_c, fac_c, scl_c, gb_c, ya_c = _even_in(ctx_s, sh_c, sc_c, w_main, w_r, j, conv_w[j],
                                                     wg3, bg, lc)
            o_c, s_ctx = _gla(v_c, fac_c, scl_c, s_zero)
            if ctx_needed:
                ctx_s = _even_out(o_c, gb_c, ya_c, norm_w, w_out_e, j, ctx_s, gt_c, lg, lb)
            v, fac, scl, gb, ya = _even_in(x, sh, sc, w_main, w_r, j, conv_w[j], wg3, bg, GRID_W)
            o, _ = _gla(v, fac, scl, s_ctx)
            x = _even_out(o, gb, ya, norm_w, w_out_e, j, x, gt, lg, lb)
        else:
            ps = pool_scale[j][None, :]
            if ctx_needed:
                m_c = _odd_mix(ctx_s, sh_c, sc_c, w_in_o, w_pool, j, ps, 1)
                ctx_s = _out_ln(m_c, w_out_o, j, ctx_s, gt_c, lg, lb)
            m = _odd_mix(x, sh, sc, w_in_o, w_pool, j, ps, GRID_W)
            x = _out_ln(m, w_out_o, j, x, gt, lg, lb)
    return x
```

```python
import functools

import jax
import jax.numpy as jnp
from jax import lax
from jax.experimental import pallas as pl
from jax.experimental.pallas import tpu as pltpu

D_MODEL = 2048
DEPTH = 4
GRID_W = 64

GLA_HEADS = 4
GLA_DK = 256
GLA_DV = 256
GLA_KEY_W = GLA_HEADS * GLA_DK
GLA_VAL_W = GLA_HEADS * GLA_DV
GLA_GATE_RANK = 16
GLA_TAU = 16.0
CONV_W = D_MODEL // 2
POOL_W = D_MODEL
POOL_WINDOWS = (2, 4, 8, 16)
POOL_GROUP = POOL_W // len(POOL_WINDOWS)
POOL_HALO_ROWS = max(POOL_WINDOWS) // 2
LN_EPS = 1e-5
ALPHA = (2.0 * DEPTH) ** 0.25

LANES = 128
COND_ROWS = 8
VMEM_LIMIT = 60000 * 1024
GLA_CHUNK = 256
GLA_BLOCK_CHUNKS = 8
SUB_TILE = 256
EVEN_OUT_TILE = 512
OUT_LN_TILE = 1024
LN_ROWS = 8
NORM_ROWS = 16
POOL_TILE = 512
RANK_COLS = 2 * GLA_GATE_RANK
STATE_W = GLA_KEY_W + GLA_VAL_W
LOG2_E = 1.4426950408889634

F32 = jnp.float32
BF16 = jnp.bfloat16

_NT = (((1,), (1,)), ((), ()))
_TN = (((0,), (0,)), ((), ()))


def _silu(t):
    return t / (1.0 + jnp.exp(-t))


def _resident(shape):
    return pl.BlockSpec(shape, lambda *_: (0,) * len(shape), pipeline_mode=pl.Buffered(1))


def _layer_resident(stacked_shape, layer):
    nd = len(stacked_shape)
    return pl.BlockSpec((None,) + tuple(stacked_shape[1:]), lambda *_: (layer,) + (0,) * (nd - 1),
                        pipeline_mode=pl.Buffered(1))


def _params(*sem):
    return pltpu.CompilerParams(dimension_semantics=sem, vmem_limit_bytes=VMEM_LIMIT)


def _modulation_kernel(c_ref, w_ref, b_ref, o_ref):
    s = _silu(c_ref[...]).astype(BF16)
    o_ref[...] = jnp.dot(s, w_ref[...].astype(BF16), preferred_element_type=F32) + b_ref[...]


def _modulation(cond, w_ada, b_ada):
    depth, d, n = w_ada.shape
    tn = 768
    return pl.pallas_call(
        _modulation_kernel,
        grid=(depth, n // tn),
        in_specs=[
            pl.BlockSpec((COND_ROWS, d), lambda i, j: (0, 0)),
            pl.BlockSpec((None, d, tn), lambda i, j: (i, 0, j)),
            pl.BlockSpec((None, 1, tn), lambda i, j: (i, 0, j)),
        ],
        out_specs=pl.BlockSpec((None, COND_ROWS, tn), lambda i, j: (i, 0, j)),
        out_shape=jax.ShapeDtypeStruct((depth, COND_ROWS, n), F32),
        compiler_params=_params("parallel", "parallel"),
        name="modulation",
    )(cond, w_ada, b_ada.reshape(depth, 1, n))


def _gate_log2(r3, wg, bg):
    z2 = jnp.dot(r3, wg, preferred_element_type=F32) + bg
    ls2 = jnp.minimum(z2, 0.0) - jnp.log2(1.0 + jnp.exp2(-jnp.abs(z2)))
    ls_hi = ls2.astype(BF16)
    ls_lo = (ls2 - ls_hi.astype(F32)).astype(BF16)
    return jnp.concatenate([ls_hi, ls_lo], axis=0)


def _running_decay(ls, fwd):
    chunk = ls.shape[0] // 2
    row = lax.broadcasted_iota(jnp.int32, (chunk, chunk), 0)
    col = lax.broadcasted_iota(jnp.int32, (chunk, chunk), 1)
    tri_b = jnp.where((row >= col) if fwd else (row <= col), 1.0 / GLA_TAU, 0.0).astype(BF16)
    return jnp.dot(jnp.concatenate([tri_b, tri_b], axis=1), ls, preferred_element_type=F32)


def _store_decay_factors(q, k, g, fwd, fac_ref, scl_ref):
    chunk = q.shape[0]
    p = g[chunk // 2:chunk // 2 + 1, :]
    g_tot = g[chunk - 1:chunk, :] if fwd else g[0:1, :]
    fac_ref[:, 0:GLA_KEY_W] = q * jnp.exp2(g - p).astype(BF16)
    fac_ref[:, GLA_KEY_W:] = k * jnp.exp2(p - g).astype(BF16)
    pad = jnp.zeros((COND_ROWS - 3, GLA_KEY_W), F32)
    scl_ref[...] = jnp.concatenate(
        [jnp.exp2(p), jnp.exp2(g_tot), jnp.exp2(g_tot - p), pad], axis=0)


def _even_in_kernel(x_ref, sh_ref, sc_ref, w_ref, wr_ref, cw_ref, wg_ref, bg_ref,
                    v_ref, fac_ref, scl_ref, gb_ref, ya_ref, *, row_w):
    tm = x_ref.shape[0]
    u = (x_ref[...] * (1.0 + sc_ref[...]) + sh_ref[...]).astype(BF16)

    def proj(col):
        return jnp.dot(u, w_ref[:, col:col + GLA_KEY_W], preferred_element_type=F32)

    r = jnp.dot(u, wr_ref[...], preferred_element_type=F32)
    r_hi = r.astype(BF16)
    r_lo = (r - r_hi.astype(F32)).astype(BF16)
    lane = lax.broadcasted_iota(jnp.int32, r.shape, 1)
    r3 = jnp.where((lane >= RANK_COLS) & (lane < 2 * RANK_COLS), r_lo, r_hi)

    k = proj(0).astype(BF16)
    ls = [_gate_log2(r3, wg_ref[d], bg_ref[d]) for d in range(2)]
    v_ref[...] = proj(1024).astype(BF16)
    q = (proj(2048) * GLA_DK ** -0.5).astype(BF16)
    g = [_running_decay(ls[d], d == 0) for d in range(2)]
    gb_ref[...] = _silu(proj(3072)).astype(BF16)
    for d in range(2):
        _store_decay_factors(q, k, g[d], d == 0, fac_ref.at[d], scl_ref.at[d])

    p = proj(5120) * proj(6144)
    t = lax.broadcasted_iota(jnp.int32, (tm, 1), 0) % row_w
    prev = jnp.where(t == 0, 0.0, pltpu.roll(p, 1, axis=0))
    nxt = jnp.where(t == row_w - 1, 0.0, pltpu.roll(p, tm - 1, axis=0))
    conv = cw_ref[0:1, :] * prev + cw_ref[1:2, :] * p + cw_ref[2:3, :] * nxt
    ya_ref[...] = (proj(4096) * conv * _silu(proj(7168))).astype(BF16)


def _even_in(x, sh, sc, w_main, w_r, layer, conv_w, wg3, bg, row_w):
    bn, L, d = x.shape
    tm = min(GLA_CHUNK, L)
    assert L % tm == 0 and tm % row_w == 0
    tile = lambda n: pl.BlockSpec((None, tm, n), lambda b, i: (b, i, 0))
    vec = pl.BlockSpec((None, 1, d), lambda b, i: (b, 0, 0))
    return pl.pallas_call(
        functools.partial(_even_in_kernel, row_w=row_w),
        grid=(bn, L // tm),
        in_specs=[tile(d), vec, vec, _layer_resident(w_main.shape, layer),
                  _layer_resident(w_r.shape, layer),
                  _resident(conv_w.shape), _resident(wg3.shape), _resident(bg.shape)],
        out_specs=[tile(GLA_VAL_W),
                   pl.BlockSpec((2, None, tm, 2 * GLA_KEY_W), lambda b, i: (0, b, i, 0)),
                   pl.BlockSpec((2, None, None, COND_ROWS, GLA_KEY_W), lambda b, i: (0, b, i, 0, 0)),
                   tile(GLA_VAL_W), tile(CONV_W)],
        out_shape=[jax.ShapeDtypeStruct((bn, L, GLA_VAL_W), BF16),
                   jax.ShapeDtypeStruct((2, bn, L, 2 * GLA_KEY_W), BF16),
                   jax.ShapeDtypeStruct((2, bn, L // tm, COND_ROWS, GLA_KEY_W), F32),
                   jax.ShapeDtypeStruct((bn, L, GLA_VAL_W), BF16),
                   jax.ShapeDtypeStruct((bn, L, CONV_W), BF16)],
        compiler_params=_params("parallel", "parallel"),
        name="even_in",
    )(x, sh, sc, w_main, w_r, conv_w, wg3, bg)


def _gla_kernel(fac_ref, scl_ref, v_ref, s0_ref, o_ref, sfin_ref, st_scr, *, chunk):
    d = pl.program_id(0)
    c = pl.program_id(2)
    n_sub = v_ref.shape[0] // chunk

    @pl.when(c == 0)
    def _():
        st_scr[...] = s0_ref[...]

    row = lax.broadcasted_iota(jnp.int32, (chunk, chunk), 0)
    col = lax.broadcasted_iota(jnp.int32, (chunk, chunk), 1)
    tri = jnp.where(d == 0, row - col, col - row) >= 0

    for s in range(n_sub):
        pos = s + d * (n_sub - 1 - 2 * s)
        rows = pl.ds(pl.multiple_of(pos * chunk, chunk), chunk)
        e_p = scl_ref[pos, 0:1, :]
        e_tot = scl_ref[pos, 1:2, :]
        e_kd = scl_ref[pos, 2:3, :].astype(BF16)
        for h in range(GLA_HEADS):
            hs = slice(h * GLA_DK, (h + 1) * GLA_DK)
            qg = fac_ref[rows, hs]
            kg = fac_ref[rows, GLA_KEY_W + h * GLA_DK:GLA_KEY_W + (h + 1) * GLA_DK]
            kd = kg * e_kd[:, hs]
            v = v_ref[rows, hs]
            a = lax.dot_general(qg, kg, _NT, preferred_element_type=F32)
            a = jnp.where(tri, a, 0.0).astype(BF16)
            st = st_scr[h]
            o = jnp.dot(a, v, preferred_element_type=F32)
            o += lax.dot_general(qg, (st * e_p[:, hs]).astype(BF16), _NT,
                                 preferred_element_type=F32)
            o_ref[rows, hs] = o.astype(BF16)
            st_scr[h] = st * e_tot[:, hs] + lax.dot_general(v, kd, _TN,
                                                            preferred_element_type=F32)

    @pl.when(c == pl.num_programs(2) - 1)
    def _():
        sfin_ref[...] = st_scr[...]


def _gla(v, fac, scl, s0):
    bn, L, _ = v.shape
    chunk = min(GLA_CHUNK, L)
    blk = min(GLA_BLOCK_CHUNKS * chunk, L)
    nc = L // blk
    assert L % blk == 0 and blk % chunk == 0

    def cidx(d, c):
        return c + d * (nc - 1 - 2 * c)

    state = pl.BlockSpec((None, None, GLA_HEADS, GLA_DV, GLA_DK), lambda d, b, c: (d, b, 0, 0, 0))
    return pl.pallas_call(
        functools.partial(_gla_kernel, chunk=chunk),
        grid=(2, bn, nc),
        in_specs=[pl.BlockSpec((None, None, blk, 2 * GLA_KEY_W),
                               lambda d, b, c: (d, b, cidx(d, c), 0)),
                  pl.BlockSpec((None, None, blk // chunk, COND_ROWS, GLA_KEY_W),
                               lambda d, b, c: (d, b, cidx(d, c), 0, 0)),
                  pl.BlockSpec((None, blk, GLA_VAL_W), lambda d, b, c: (b, cidx(d, c), 0)),
                  state],
        out_specs=[pl.BlockSpec((None, None, blk, GLA_VAL_W), lambda d, b, c: (d, b, cidx(d, c), 0)),
                   state],
        out_shape=[jax.ShapeDtypeStruct((2, bn, L, GLA_VAL_W), BF16),
                   jax.ShapeDtypeStruct((2, bn, GLA_HEADS, GLA_DV, GLA_DK), F32)],
        scratch_shapes=[pltpu.VMEM((GLA_HEADS, GLA_DV, GLA_DK), F32)],
        compiler_params=_params("parallel", "parallel", "arbitrary"),
        name="gla",
    )(fac, scl, v, s0)


def _residual_ln_rows(x_ref, out_ref, row0, y, gts, g, b):
    for r in range(0, y.shape[0], LN_ROWS):
        rows = pl.ds(row0 + r, LN_ROWS)
        w = x_ref[rows, :] + gts * y[r:r + LN_ROWS, :]
        mu = jnp.mean(w, axis=-1, keepdims=True)
        xc = w - mu
        var = jnp.mean(xc * xc, axis=-1, keepdims=True)
        out_ref[rows, :] = xc * lax.rsqrt(var + LN_EPS / ALPHA ** 2) * g + b


def _gla_branch_rows(of_ref, ob_ref, gb_ref, nw, row0, yb_ref):
    for r in range(0, SUB_TILE, NORM_ROWS):
        rows = pl.ds(row0 + r, NORM_ROWS)
        o = of_ref[rows, :].astype(F32) + ob_ref[rows, :].astype(F32)
        heads = []
        for h in range(GLA_HEADS):
            oh = o[:, h * GLA_DV:(h + 1) * GLA_DV]
            ms = jnp.mean(oh * oh, axis=-1, keepdims=True)
            heads.append(oh * lax.rsqrt(ms + LN_EPS) * nw)
        yb = jnp.concatenate(heads, axis=-1) * gb_ref[rows, :].astype(F32)
        yb_ref[r:r + NORM_ROWS, :] = yb.astype(BF16)


def _even_out_kernel(of_ref, ob_ref, gb_ref, ya_ref, nw_ref, w_ref, x_ref, gt_ref, lg_ref, lb_ref,
                     out_ref, yb_scr):
    n_sub = x_ref.shape[0] // SUB_TILE
    sub = lambda s: pl.ds(s * SUB_TILE, SUB_TILE)
    gts = gt_ref[...] * (1.0 / ALPHA)
    nw, lg, lb = nw_ref[...], lg_ref[...], lb_ref[...]
    _gla_branch_rows(of_ref, ob_ref, gb_ref, nw, 0, yb_scr.at[0])
    y = None
    for s in range(n_sub + 1):
        y_prev = y
        if s < n_sub:
            y = jnp.dot(yb_scr[s % 2], w_ref[0:GLA_VAL_W, :], preferred_element_type=F32)
            y += jnp.dot(ya_ref[sub(s), :], w_ref[GLA_VAL_W:, :], preferred_element_type=F32)
        if s + 1 < n_sub:
            _gla_branch_rows(of_ref, ob_ref, gb_ref, nw, (s + 1) * SUB_TILE, yb_scr.at[(s + 1) % 2])
        if s > 0:
            _residual_ln_rows(x_ref, out_ref, (s - 1) * SUB_TILE, y_prev, gts, lg, lb)


def _even_out(o, gb, ya, norm_w, w_out, layer, x, gt, ln_g, ln_b):
    bn, L, d = x.shape
    tm = min(EVEN_OUT_TILE, L)
    assert L % tm == 0 and tm % SUB_TILE == 0
    tile = lambda n: pl.BlockSpec((None, tm, n), lambda b, i: (b, i, 0))
    odir = lambda k: pl.BlockSpec((None, None, tm, GLA_VAL_W), lambda b, i: (k, b, i, 0))
    return pl.pallas_call(
        _even_out_kernel,
        grid=(bn, L // tm),
        in_specs=[odir(0), odir(1), tile(GLA_VAL_W), tile(CONV_W), _resident(norm_w.shape),
                  _layer_resident(w_out.shape, layer), tile(d),
                  pl.BlockSpec((None, 1, d), lambda b, i: (b, 0, 0)),
                  _resident(ln_g.shape), _resident(ln_b.shape)],
        out_specs=tile(d),
        out_shape=jax.ShapeDtypeStruct((bn, L, d), F32),
        scratch_shapes=[pltpu.VMEM((2, SUB_TILE, GLA_VAL_W), BF16)],
        compiler_params=_params("parallel", "parallel"),
        name="even_out",
    )(o, o, gb, ya, norm_w, w_out, x, gt, ln_g, ln_b)


def _out_ln_kernel(m_ref, w_ref, x_ref, gt_ref, lg_ref, lb_ref, out_ref):
    n_sub = x_ref.shape[0] // SUB_TILE
    sub = lambda s: pl.ds(s * SUB_TILE, SUB_TILE)
    gts = gt_ref[...] * (1.0 / ALPHA)
    lg, lb = lg_ref[...], lb_ref[...]
    y = None
    for s in range(n_sub + 1):
        y_prev = y
        if s < n_sub:
            y = jnp.dot(m_ref[sub(s), :], w_ref[...], preferred_element_type=F32)
        if s > 0:
            _residual_ln_rows(x_ref, out_ref, (s - 1) * SUB_TILE, y_prev, gts, lg, lb)


def _out_ln(m, w_out, layer, x, gt, ln_g, ln_b):
    bn, L, d = x.shape
    tm = min(OUT_LN_TILE, L)
    assert L % tm == 0 and tm % SUB_TILE == 0
    tile = lambda n: pl.BlockSpec((None, tm, n), lambda b, i: (b, i, 0))
    return pl.pallas_call(
        _out_ln_kernel,
        grid=(bn, L // tm),
        in_specs=[tile(m.shape[-1]), _layer_resident(w_out.shape, layer), tile(d),
                  pl.BlockSpec((None, 1, d), lambda b, i: (b, 0, 0)),
                  _resident(ln_g.shape), _resident(ln_b.shape)],
        out_specs=tile(d),
        out_shape=jax.ShapeDtypeStruct((bn, L, d), F32),
        compiler_params=_params("parallel", "parallel"),
        name="out_ln",
    )(m, w_out, x, gt, ln_g, ln_b)


def _odd_mix_kernel(xp_ref, xc_ref, xn_ref, sh_ref, sc_ref, win_ref, wpool_ref, ps_ref, m_ref,
                    *, row_stride, n_rows):
    i = pl.program_id(1)
    tm = xc_ref.shape[0]
    halo = POOL_HALO_ROWS * row_stride
    xc = xc_ref[...]
    scale1 = 1.0 + sc_ref[...]
    u = (xc * scale1 + sh_ref[...]).astype(BF16)
    sgate = _silu(jnp.dot(u, win_ref[:, POOL_W:], preferred_element_type=F32)).astype(BF16)

    before = jnp.where(i > 0, xp_ref[tm - halo:, :], 0.0)
    after = jnp.where(i < pl.num_programs(1) - 1, xn_ref[:halo, :], 0.0)
    xe = jnp.concatenate([before, xc, after], axis=0)
    grid_row = (i * tm + lax.broadcasted_iota(jnp.int32, (tm, 1), 0)) // row_stride

    def window_diff(s, w):
        half = w // 2
        s = s[:s.shape[0] - half * row_stride, :] + s[half * row_stride:, :]
        start = halo - half * row_stride
        cnt = jnp.minimum(grid_row + half, n_rows) - jnp.maximum(grid_row - half, 0)
        mean = s[start:start + tm, :] / cnt.astype(F32)
        return s, ((mean - xc) * scale1).astype(BF16)

    s, diff = window_diff(xe, POOL_WINDOWS[0])
    for g in range(len(POOL_WINDOWS)):
        cur = diff
        if g + 1 < len(POOL_WINDOWS):
            s, diff = window_diff(s, POOL_WINDOWS[g + 1])
        cols = slice(g * POOL_GROUP, (g + 1) * POOL_GROUP)
        z = jnp.dot(cur, win_ref[:, cols], preferred_element_type=F32)
        z = jnp.dot(z.astype(BF16), wpool_ref[g], preferred_element_type=F32)
        m_ref[:, cols] = (z * ps_ref[:, cols] * sgate[:, cols].astype(F32)).astype(BF16)


def _odd_mix(x, sh, sc, w_in, w_pool, layer, pool_scale, row_stride):
    bn, L, d = x.shape
    n_rows = L // row_stride
    tm = min(POOL_TILE, L)
    nt = L // tm
    assert L % tm == 0 and tm % row_stride == 0 and POOL_HALO_ROWS * row_stride <= tm
    vec = pl.BlockSpec((None, 1, d), lambda b, i: (b, 0, 0))
    return pl.pallas_call(
        functools.partial(_odd_mix_kernel, row_stride=row_stride, n_rows=n_rows),
        grid=(bn, nt),
        in_specs=[pl.BlockSpec((None, tm, d), lambda b, i: (b, jnp.maximum(i - 1, 0), 0)),
                  pl.BlockSpec((None, tm, d), lambda b, i: (b, i, 0)),
                  pl.BlockSpec((None, tm, d), lambda b, i: (b, jnp.minimum(i + 1, nt - 1), 0)),
                  vec, vec, _layer_resident(w_in.shape, layer), _layer_resident(w_pool.shape, layer),
                  _resident(pool_scale.shape)],
        out_specs=pl.BlockSpec((None, tm, POOL_W), lambda b, i: (b, i, 0)),
        out_shape=jax.ShapeDtypeStruct((bn, L, POOL_W), BF16),
        compiler_params=_params("parallel", "parallel"),
        name="odd_mix",
    )(x, x, x, sh, sc, w_in, w_pool, pool_scale)


def _hi_lo(w):
    hi = w.astype(BF16)
    return hi, (w - hi.astype(F32)).astype(BF16)


def _even_w_prep_kernel(wt_ref, rank_ref, main_ref, r_ref):
    main_ref[...] = wt_ref[0].T.astype(BF16)
    rank = rank_ref[0].T[:, LANES - RANK_COLS:].astype(BF16)
    pad = jnp.zeros((rank.shape[0], LANES - 3 * RANK_COLS), BF16)
    r_ref[...] = jnp.concatenate([rank, rank, rank, pad], axis=1)


def _even_w_prep(w_in_e):
    ne, d, n = w_in_e.shape
    tn = 512
    assert STATE_W % tn == 0 and (n - RANK_COLS) % tn == 0
    wt = jnp.transpose(w_in_e, (0, 2, 1))

    def src_row(j, b):
        row = b * tn + jnp.where(b * tn >= STATE_W, RANK_COLS, 0)
        return (j, pl.multiple_of(row, RANK_COLS), 0)

    return pl.pallas_call(
        _even_w_prep_kernel,
        grid=(ne, (n - RANK_COLS) // tn),
        in_specs=[pl.BlockSpec((pl.Element(1), pl.Element(tn), pl.Element(d)), src_row),
                  pl.BlockSpec((pl.Element(1), pl.Element(LANES), pl.Element(d)),
                               lambda j, b: (j, STATE_W + RANK_COLS - LANES, 0))],
        out_specs=[pl.BlockSpec((None, d, tn), lambda j, b: (j, 0, b)),
                   pl.BlockSpec((None, d, LANES), lambda j, b: (j, 0, 0))],
        out_shape=[jax.ShapeDtypeStruct((ne, d, n - RANK_COLS), BF16),
                   jax.ShapeDtypeStruct((ne, d, LANES), BF16)],
        compiler_params=_params("parallel", "arbitrary"),
        name="even_w_prep",
    )(wt, wt)


def _gate_weights(w_gf, b_gf, w_gb, b_gb):
    zr = jnp.zeros((GLA_GATE_RANK, GLA_KEY_W), BF16)
    tail = jnp.zeros((LANES - 3 * RANK_COLS, GLA_KEY_W), BF16)
    f_hi, f_lo = _hi_lo(w_gf * LOG2_E)
    b_hi, b_lo = _hi_lo(w_gb * LOG2_E)
    wg3 = jnp.stack([jnp.concatenate([f_hi, zr, f_hi, zr, f_lo, zr, tail], axis=0),
                     jnp.concatenate([zr, b_hi, zr, b_hi, zr, b_lo, tail], axis=0)])
    bg = jnp.stack([b_gf, b_gb])[:, None, :] * LOG2_E
    return wg3, bg


def kernel(x, c, ctx, c_ctx, w_ada, b_ada, ln_g, ln_b, w_in_e, w_gate_f, b_gate_f, w_gate_b,
           b_gate_b, gla_norm_w, conv_w, w_out_e, w_in_o, w_pool, pool_scale, w_out_o):
    bn, L, d = x.shape
    lc = ctx.shape[1]
    cond = jnp.concatenate([c, c_ctx[None, :], jnp.zeros((COND_ROWS - bn - 1, d), F32)], axis=0)
    mod = _modulation(cond, w_ada, b_ada)
    s_zero = jnp.zeros((2, bn, GLA_HEADS, GLA_DV, GLA_DK), F32)
    w_main, w_r = _even_w_prep(w_in_e)
    w_out_e, w_in_o, w_out_o, w_pool = (
        w.astype(BF16) for w in (w_out_e, w_in_o, w_out_o, w_pool))
    ctx_s = ctx
    for i in range(DEPTH):
        j = i // 2
        ctx_needed = any(l % 2 == 0 for l in range(i + 1, DEPTH))
        sh, sc, gt = (mod[i, :bn, n * d:(n + 1) * d][:, None, :] for n in range(3))
        sh_c, sc_c, gt_c = (jnp.broadcast_to(mod[i, bn, n * d:(n + 1) * d], (bn, 1, d))
                            for n in range(3))
        lg, lb = ln_g[i][None, :], ln_b[i][None, :]
        if i % 2 == 0:
            wg3, bg = _gate_weights(w_gate_f[j], b_gate_f[j], w_gate_b[j], b_gate_b[j])
            norm_w = gla_norm_w[j][None, :]
            v_c, fac_c, scl_c, gb_c, ya_c = _even_in(ctx_s, sh_c, sc_c, w_main, w_r, j, conv_w[j],
                                                     wg3, bg, lc)
            o_c, s_ctx = _gla(v_c, fac_c, scl_c, s_zero)
            if ctx_needed:
                ctx_s = _even_out(o_c, gb_c, ya_c, norm_w, w_out_e, j, ctx_s, gt_c, lg, lb)
            v, fac, scl, gb, ya = _even_in(x, sh, sc, w_main, w_r, j, conv_w[j], wg3, bg, GRID_W)
            o, _ = _gla(v, fac, scl, s_ctx)
            x = _even_out(o, gb, ya, norm_w, w_out_e, j, x, gt, lg, lb)
        else:
            ps = pool_scale[j][None, :]
            if ctx_needed:
                m_c = _odd_mix(ctx_s, sh_c, sc_c, w_in_o, w_pool, j, ps, 1)
                ctx_s = _out_ln(m_c, w_out_o, j, ctx_s, gt_c, lg, lb)
            m = _odd_mix(x, sh, sc, w_in_o, w_pool, j, ps, GRID_W)
            x = _out_ln(m, w_out_o, j, x, gt, lg, lb)
    return x
```

```python
import functools

import jax
import jax.numpy as jnp
from jax import lax
from jax.experimental import pallas as pl
from jax.experimental.pallas import tpu as pltpu

D_MODEL = 2048
DEPTH = 4
GRID_W = 64

GLA_HEADS = 4
GLA_DK = 256
GLA_DV = 256
GLA_KEY_W = GLA_HEADS * GLA_DK
GLA_VAL_W = GLA_HEADS * GLA_DV
GLA_GATE_RANK = 16
GLA_TAU = 16.0
CONV_W = D_MODEL // 2
POOL_W = D_MODEL
POOL_WINDOWS = (2, 4, 8, 16)
POOL_GROUP = POOL_W // len(POOL_WINDOWS)
POOL_HALO_ROWS = max(POOL_WINDOWS) // 2
LN_EPS = 1e-5
ALPHA = (2.0 * DEPTH) ** 0.25

LANES = 128
COND_ROWS = 8
VMEM_LIMIT = 60000 * 1024
GLA_CHUNK = 256
GLA_BLOCK_CHUNKS = 8
SUB_TILE = 256
EVEN_OUT_TILE = 512
OUT_LN_TILE = 1024
LN_ROWS = 8
NORM_ROWS = 16
POOL_TILE = 512
RANK_COLS = 2 * GLA_GATE_RANK
STATE_W = GLA_KEY_W + GLA_VAL_W
MAIN_BLOCKS = ("k", "v", "q", "gate_b", "a_b", "a_c", "a_x", "gate_a")
MOD_COLS = 1536
PREP_COLS = 1024
LOG2_E = 1.4426950408889634

F32 = jnp.float32
BF16 = jnp.bfloat16

_NT = (((1,), (1,)), ((), ()))
_TN = (((0,), (0,)), ((), ()))


def _silu(t):
    return t / (1.0 + jnp.exp(-t))


def _resident(shape):
    return pl.BlockSpec(shape, lambda *_: (0,) * len(shape), pipeline_mode=pl.Buffered(1))


def _layer_resident(stacked_shape, layer):
    nd = len(stacked_shape)
    return pl.BlockSpec((None,) + tuple(stacked_shape[1:]), lambda *_: (layer,) + (0,) * (nd - 1),
                        pipeline_mode=pl.Buffered(1))


def _params(*sem):
    return pltpu.CompilerParams(dimension_semantics=sem, vmem_limit_bytes=VMEM_LIMIT)


def _modulation_kernel(c_ref, w_ref, b_ref, o_ref):
    s = _silu(c_ref[...]).astype(BF16)
    o_ref[...] = jnp.dot(s, w_ref[...].astype(BF16), preferred_element_type=F32) + b_ref[...]


def _modulation(cond, w_ada, b_ada):
    depth, d, n = w_ada.shape
    tn = MOD_COLS
    return pl.pallas_call(
        _modulation_kernel,
        grid=(depth, n // tn),
        in_specs=[
            pl.BlockSpec((COND_ROWS, d), lambda i, j: (0, 0)),
            pl.BlockSpec((None, d, tn), lambda i, j: (i, 0, j)),
            pl.BlockSpec((None, 1, tn), lambda i, j: (i, 0, j)),
        ],
        out_specs=pl.BlockSpec((None, COND_ROWS, tn), lambda i, j: (i, 0, j)),
        out_shape=jax.ShapeDtypeStruct((depth, COND_ROWS, n), F32),
        compiler_params=_params("parallel", "parallel"),
        name="modulation",
    )(cond, w_ada, b_ada.reshape(depth, 1, n))


def _gate_log2(r3, wg, bg):
    z2 = jnp.dot(r3, wg, preferred_element_type=F32) + bg
    ls2 = jnp.minimum(z2, 0.0) - jnp.log2(1.0 + jnp.exp2(-jnp.abs(z2)))
    ls_hi = ls2.astype(BF16)
    ls_lo = (ls2 - ls_hi.astype(F32)).astype(BF16)
    return jnp.concatenate([ls_hi, ls_lo], axis=0)


def _running_decay(ls, fwd):
    chunk = ls.shape[0] // 2
    row = lax.broadcasted_iota(jnp.int32, (chunk, chunk), 0)
    col = lax.broadcasted_iota(jnp.int32, (chunk, chunk), 1)
    tri_b = jnp.where((row >= col) if fwd else (row <= col), 1.0 / GLA_TAU, 0.0).astype(BF16)
    return jnp.dot(jnp.concatenate([tri_b, tri_b], axis=1), ls, preferred_element_type=F32)


def _store_decay_factors(q, k, g, fwd, fac_ref, scl_ref):
    chunk = q.shape[0]
    p = g[chunk // 2:chunk // 2 + 1, :]
    g_tot = g[chunk - 1:chunk, :] if fwd else g[0:1, :]
    fac_ref[:, 0:GLA_KEY_W] = q * jnp.exp2(g - p).astype(BF16)
    fac_ref[:, GLA_KEY_W:] = k * jnp.exp2(p - g).astype(BF16)
    pad = jnp.zeros((COND_ROWS - 3, GLA_KEY_W), F32)
    scl_ref[...] = jnp.concatenate(
        [jnp.exp2(p), jnp.exp2(g_tot), jnp.exp2(g_tot - p), pad], axis=0)


def _even_in_kernel(x_ref, sh_ref, sc_ref, w_ref, wr_ref, cw_ref, wg_ref, bg_ref,
                    v_ref, fac_ref, scl_ref, gb_ref, ya_ref, *, row_w):
    tm = x_ref.shape[0]
    u = (x_ref[...] * (1.0 + sc_ref[...]) + sh_ref[...]).astype(BF16)

    def proj(name):
        col = MAIN_BLOCKS.index(name) * GLA_KEY_W
        return jnp.dot(u, w_ref[:, col:col + GLA_KEY_W], preferred_element_type=F32)

    r = jnp.dot(u, wr_ref[...], preferred_element_type=F32)
    r_hi = r.astype(BF16)
    r_lo = (r - r_hi.astype(F32)).astype(BF16)
    lane = lax.broadcasted_iota(jnp.int32, r.shape, 1)
    r3 = jnp.where((lane >= RANK_COLS) & (lane < 2 * RANK_COLS), r_lo, r_hi)

    k = proj("k").astype(BF16)
    ls = [_gate_log2(r3, wg_ref[d], bg_ref[d]) for d in range(2)]
    v_ref[...] = proj("v").astype(BF16)
    q = (proj("q") * GLA_DK ** -0.5).astype(BF16)
    g = [_running_decay(ls[d], d == 0) for d in range(2)]
    gb_ref[...] = _silu(proj("gate_b")).astype(BF16)
    for d in range(2):
        _store_decay_factors(q, k, g[d], d == 0, fac_ref.at[d], scl_ref.at[d])

    p = proj("a_c") * proj("a_x")
    t = lax.broadcasted_iota(jnp.int32, (tm, 1), 0) % row_w
    prev = jnp.where(t == 0, 0.0, pltpu.roll(p, 1, axis=0))
    nxt = jnp.where(t == row_w - 1, 0.0, pltpu.roll(p, tm - 1, axis=0))
    conv = cw_ref[0:1, :] * prev + cw_ref[1:2, :] * p + cw_ref[2:3, :] * nxt
    ya_ref[...] = (proj("a_b") * conv * _silu(proj("gate_a"))).astype(BF16)


def _even_in(x, sh, sc, w_main, w_r, layer, conv_w, wg3, bg, row_w):
    bn, L, d = x.shape
    tm = min(GLA_CHUNK, L)
    assert L % tm == 0 and tm % row_w == 0
    tile = lambda n: pl.BlockSpec((None, tm, n), lambda b, i: (b, i, 0))
    vec = pl.BlockSpec((None, 1, d), lambda b, i: (b, 0, 0))
    return pl.pallas_call(
        functools.partial(_even_in_kernel, row_w=row_w),
        grid=(bn, L // tm),
        in_specs=[tile(d), vec, vec, _layer_resident(w_main.shape, layer),
                  _layer_resident(w_r.shape, layer),
                  _resident(conv_w.shape), _resident(wg3.shape), _resident(bg.shape)],
        out_specs=[tile(GLA_VAL_W),
                   pl.BlockSpec((2, None, tm, 2 * GLA_KEY_W), lambda b, i: (0, b, i, 0)),
                   pl.BlockSpec((2, None, None, COND_ROWS, GLA_KEY_W), lambda b, i: (0, b, i, 0, 0)),
                   tile(GLA_VAL_W), tile(CONV_W)],
        out_shape=[jax.ShapeDtypeStruct((bn, L, GLA_VAL_W), BF16),
                   jax.ShapeDtypeStruct((2, bn, L, 2 * GLA_KEY_W), BF16),
                   jax.ShapeDtypeStruct((2, bn, L // tm, COND_ROWS, GLA_KEY_W), F32),
                   jax.ShapeDtypeStruct((bn, L, GLA_VAL_W), BF16),
                   jax.ShapeDtypeStruct((bn, L, CONV_W), BF16)],
        compiler_params=_params("parallel", "parallel"),
        name="even_in",
    )(x, sh, sc, w_main, w_r, conv_w, wg3, bg)


def _gla_kernel(fac_ref, scl_ref, v_ref, s0_ref, o_ref, sfin_ref, st_scr, *, chunk):
    d = pl.program_id(0)
    c = pl.program_id(2)
    n_sub = v_ref.shape[0] // chunk

    @pl.when(c == 0)
    def _():
        st_scr[...] = s0_ref[...]

    row = lax.broadcasted_iota(jnp.int32, (chunk, chunk), 0)
    col = lax.broadcasted_iota(jnp.int32, (chunk, chunk), 1)
    tri = jnp.where(d == 0, row - col, col - row) >= 0

    for s in range(n_sub):
        pos = s + d * (n_sub - 1 - 2 * s)
        rows = pl.ds(pl.multiple_of(pos * chunk, chunk), chunk)
        e_p = scl_ref[pos, 0:1, :]
        e_tot = scl_ref[pos, 1:2, :]
        e_kd = scl_ref[pos, 2:3, :].astype(BF16)
        for h in range(GLA_HEADS):
            hs = slice(h * GLA_DK, (h + 1) * GLA_DK)
            qg = fac_ref[rows, hs]
            kg = fac_ref[rows, GLA_KEY_W + h * GLA_DK:GLA_KEY_W + (h + 1) * GLA_DK]
            kd = kg * e_kd[:, hs]
            v = v_ref[rows, hs]
            a = lax.dot_general(qg, kg, _NT, preferred_element_type=F32)
            a = jnp.where(tri, a, 0.0).astype(BF16)
            st = st_scr[h]
            o = jnp.dot(a, v, preferred_element_type=F32)
            o += lax.dot_general(qg, (st * e_p[:, hs]).astype(BF16), _NT,
                                 preferred_element_type=F32)
            o_ref[rows, hs] = o.astype(BF16)
            st_scr[h] = st * e_tot[:, hs] + lax.dot_general(v, kd, _TN,
                                                            preferred_element_type=F32)

    @pl.when(c == pl.num_programs(2) - 1)
    def _():
        sfin_ref[...] = st_scr[...]


def _gla(v, fac, scl, s0):
    bn, L, _ = v.shape
    chunk = min(GLA_CHUNK, L)
    blk = min(GLA_BLOCK_CHUNKS * chunk, L)
    nc = L // blk
    assert L % blk == 0 and blk % chunk == 0

    def cidx(d, c):
        return c + d * (nc - 1 - 2 * c)

    state = pl.BlockSpec((None, None, GLA_HEADS, GLA_DV, GLA_DK), lambda d, b, c: (d, b, 0, 0, 0))
    return pl.pallas_call(
        functools.partial(_gla_kernel, chunk=chunk),
        grid=(2, bn, nc),
        in_specs=[pl.BlockSpec((None, None, blk, 2 * GLA_KEY_W),
                               lambda d, b, c: (d, b, cidx(d, c), 0)),
                  pl.BlockSpec((None, None, blk // chunk, COND_ROWS, GLA_KEY_W),
                               lambda d, b, c: (d, b, cidx(d, c), 0, 0)),
                  pl.BlockSpec((None, blk, GLA_VAL_W), lambda d, b, c: (b, cidx(d, c), 0)),
                  state],
        out_specs=[pl.BlockSpec((None, None, blk, GLA_VAL_W), lambda d, b, c: (d, b, cidx(d, c), 0)),
                   state],
        out_shape=[jax.ShapeDtypeStruct((2, bn, L, GLA_VAL_W), BF16),
                   jax.ShapeDtypeStruct((2, bn, GLA_HEADS, GLA_DV, GLA_DK), F32)],
        scratch_shapes=[pltpu.VMEM((GLA_HEADS, GLA_DV, GLA_DK), F32)],
        compiler_params=_params("parallel", "parallel", "arbitrary"),
        name="gla",
    )(fac, scl, v, s0)


def _residual_ln_rows(x_ref, out_ref, row0, y, gts, g, b):
    for r in range(0, y.shape[0], LN_ROWS):
        rows = pl.ds(row0 + r, LN_ROWS)
        w = x_ref[rows, :] + gts * y[r:r + LN_ROWS, :]
        mu = jnp.mean(w, axis=-1, keepdims=True)
        xc = w - mu
        var = jnp.mean(xc * xc, axis=-1, keepdims=True)
        out_ref[rows, :] = xc * lax.rsqrt(var + LN_EPS / ALPHA ** 2) * g + b


def _gla_branch_rows(of_ref, ob_ref, gb_ref, nw, row0, yb_ref):
    for r in range(0, SUB_TILE, NORM_ROWS):
        rows = pl.ds(row0 + r, NORM_ROWS)
        o = of_ref[rows, :].astype(F32) + ob_ref[rows, :].astype(F32)
        heads = []
        for h in range(GLA_HEADS):
            oh = o[:, h * GLA_DV:(h + 1) * GLA_DV]
            ms = jnp.mean(oh * oh, axis=-1, keepdims=True)
            heads.append(oh * lax.rsqrt(ms + LN_EPS) * nw)
        yb = jnp.concatenate(heads, axis=-1) * gb_ref[rows, :].astype(F32)
        yb_ref[r:r + NORM_ROWS, :] = yb.astype(BF16)


def _even_out_kernel(of_ref, ob_ref, gb_ref, ya_ref, nw_ref, w_ref, x_ref, gt_ref, lg_ref, lb_ref,
                     out_ref, yb_scr):
    n_sub = x_ref.shape[0] // SUB_TILE
    sub = lambda s: pl.ds(s * SUB_TILE, SUB_TILE)
    gts = gt_ref[...] * (1.0 / ALPHA)
    nw, lg, lb = nw_ref[...], lg_ref[...], lb_ref[...]
    _gla_branch_rows(of_ref, ob_ref, gb_ref, nw, 0, yb_scr.at[0])
    y = None
    for s in range(n_sub + 1):
        y_prev = y
        if s < n_sub:
            y = jnp.dot(yb_scr[s % 2], w_ref[0:GLA_VAL_W, :], preferred_element_type=F32)
            y += jnp.dot(ya_ref[sub(s), :], w_ref[GLA_VAL_W:, :], preferred_element_type=F32)
        if s + 1 < n_sub:
            _gla_branch_rows(of_ref, ob_ref, gb_ref, nw, (s + 1) * SUB_TILE, yb_scr.at[(s + 1) % 2])
        if s > 0:
            _residual_ln_rows(x_ref, out_ref, (s - 1) * SUB_TILE, y_prev, gts, lg, lb)


def _even_out(o, gb, ya, norm_w, w_out, layer, x, gt, ln_g, ln_b):
    bn, L, d = x.shape
    tm = min(EVEN_OUT_TILE, L)
    assert L % tm == 0 and tm % SUB_TILE == 0
    tile = lambda n: pl.BlockSpec((None, tm, n), lambda b, i: (b, i, 0))
    odir = lambda k: pl.BlockSpec((None, None, tm, GLA_VAL_W), lambda b, i: (k, b, i, 0))
    return pl.pallas_call(
        _even_out_kernel,
        grid=(bn, L // tm),
        in_specs=[odir(0), odir(1), tile(GLA_VAL_W), tile(CONV_W), _resident(norm_w.shape),
                  _layer_resident(w_out.shape, layer), tile(d),
                  pl.BlockSpec((None, 1, d), lambda b, i: (b, 0, 0)),
                  _resident(ln_g.shape), _resident(ln_b.shape)],
        out_specs=tile(d),
        out_shape=jax.ShapeDtypeStruct((bn, L, d), F32),
        scratch_shapes=[pltpu.VMEM((2, SUB_TILE, GLA_VAL_W), BF16)],
        compiler_params=_params("parallel", "parallel"),
        name="even_out",
    )(o, o, gb, ya, norm_w, w_out, x, gt, ln_g, ln_b)


def _out_ln_kernel(m_ref, w_ref, x_ref, gt_ref, lg_ref, lb_ref, out_ref):
    n_sub = x_ref.shape[0] // SUB_TILE
    sub = lambda s: pl.ds(s * SUB_TILE, SUB_TILE)
    gts = gt_ref[...] * (1.0 / ALPHA)
    lg, lb = lg_ref[...], lb_ref[...]
    y = None
    for s in range(n_sub + 1):
        y_prev = y
        if s < n_sub:
            y = jnp.dot(m_ref[sub(s), :], w_ref[...], preferred_element_type=F32)
        if s > 0:
            _residual_ln_rows(x_ref, out_ref, (s - 1) * SUB_TILE, y_prev, gts, lg, lb)


def _out_ln(m, w_out, layer, x, gt, ln_g, ln_b):
    bn, L, d = x.shape
    tm = min(OUT_LN_TILE, L)
    assert L % tm == 0 and tm % SUB_TILE == 0
    tile = lambda n: pl.BlockSpec((None, tm, n), lambda b, i: (b, i, 0))
    return pl.pallas_call(
        _out_ln_kernel,
        grid=(bn, L // tm),
        in_specs=[tile(m.shape[-1]), _layer_resident(w_out.shape, layer), tile(d),
                  pl.BlockSpec((None, 1, d), lambda b, i: (b, 0, 0)),
                  _resident(ln_g.shape), _resident(ln_b.shape)],
        out_specs=tile(d),
        out_shape=jax.ShapeDtypeStruct((bn, L, d), F32),
        compiler_params=_params("parallel", "parallel"),
        name="out_ln",
    )(m, w_out, x, gt, ln_g, ln_b)


def _odd_mix_kernel(xp_ref, xc_ref, xn_ref, sh_ref, sc_ref, win_ref, wpool_ref, ps_ref, m_ref,
                    *, row_stride, n_rows):
    i = pl.program_id(1)
    tm = xc_ref.shape[0]
    halo = POOL_HALO_ROWS * row_stride
    xc = xc_ref[...]
    scale1 = 1.0 + sc_ref[...]
    u = (xc * scale1 + sh_ref[...]).astype(BF16)
    sgate = _silu(jnp.dot(u, win_ref[:, POOL_W:], preferred_element_type=F32)).astype(BF16)

    before = jnp.where(i > 0, xp_ref[tm - halo:, :], 0.0)
    after = jnp.where(i < pl.num_programs(1) - 1, xn_ref[:halo, :], 0.0)
    xe = jnp.concatenate([before, xc, after], axis=0)
    grid_row = (i * tm + lax.broadcasted_iota(jnp.int32, (tm, 1), 0)) // row_stride

    def window_diff(s, w):
        half = w // 2
        s = s[:s.shape[0] - half * row_stride, :] + s[half * row_stride:, :]
        start = halo - half * row_stride
        cnt = jnp.minimum(grid_row + half, n_rows) - jnp.maximum(grid_row - half, 0)
        mean = s[start:start + tm, :] / cnt.astype(F32)
        return s, ((mean - xc) * scale1).astype(BF16)

    s, diff = window_diff(xe, POOL_WINDOWS[0])
    for g in range(len(POOL_WINDOWS)):
        cur = diff
        if g + 1 < len(POOL_WINDOWS):
            s, diff = window_diff(s, POOL_WINDOWS[g + 1])
        cols = slice(g * POOL_GROUP, (g + 1) * POOL_GROUP)
        z = jnp.dot(cur, win_ref[:, cols], preferred_element_type=F32)
        z = jnp.dot(z.astype(BF16), wpool_ref[g], preferred_element_type=F32)
        m_ref[:, cols] = (z * ps_ref[:, cols] * sgate[:, cols].astype(F32)).astype(BF16)


def _odd_mix(x, sh, sc, w_in, w_pool, layer, pool_scale, row_stride):
    bn, L, d = x.shape
    n_rows = L // row_stride
    tm = min(POOL_TILE, L)
    nt = L // tm
    assert L % tm == 0 and tm % row_stride == 0 and POOL_HALO_ROWS * row_stride <= tm
    vec = pl.BlockSpec((None, 1, d), lambda b, i: (b, 0, 0))
    return pl.pallas_call(
        functools.partial(_odd_mix_kernel, row_stride=row_stride, n_rows=n_rows),
        grid=(bn, nt),
        in_specs=[pl.BlockSpec((None, tm, d), lambda b, i: (b, jnp.maximum(i - 1, 0), 0)),
                  pl.BlockSpec((None, tm, d), lambda b, i: (b, i, 0)),
                  pl.BlockSpec((None, tm, d), lambda b, i: (b, jnp.minimum(i + 1, nt - 1), 0)),
                  vec, vec, _layer_resident(w_in.shape, layer), _layer_resident(w_pool.shape, layer),
                  _resident(pool_scale.shape)],
        out_specs=pl.BlockSpec((None, tm, POOL_W), lambda b, i: (b, i, 0)),
        out_shape=jax.ShapeDtypeStruct((bn, L, POOL_W), BF16),
        compiler_params=_params("parallel", "parallel"),
        name="odd_mix",
    )(x, x, x, sh, sc, w_in, w_pool, pool_scale)


def _hi_lo(w):
    hi = w.astype(BF16)
    return hi, (w - hi.astype(F32)).astype(BF16)


def _even_w_prep_kernel(wt_ref, rank_ref, main_ref, r_ref):
    main_ref[...] = wt_ref[0].T.astype(BF16)
    rank = rank_ref[0].T[:, LANES - RANK_COLS:].astype(BF16)
    pad = jnp.zeros((rank.shape[0], LANES - 3 * RANK_COLS), BF16)
    r_ref[...] = jnp.concatenate([rank, rank, rank, pad], axis=1)


def _even_w_prep(w_in_e):
    ne, d, n = w_in_e.shape
    tn = PREP_COLS
    assert STATE_W % tn == 0 and (n - RANK_COLS) % tn == 0
    wt = jnp.transpose(w_in_e, (0, 2, 1))

    def src_row(j, b):
        row = b * tn + jnp.where(b * tn >= STATE_W, RANK_COLS, 0)
        return (j, pl.multiple_of(row, RANK_COLS), 0)

    return pl.pallas_call(
        _even_w_prep_kernel,
        grid=(ne, (n - RANK_COLS) // tn),
        in_specs=[pl.BlockSpec((pl.Element(1), pl.Element(tn), pl.Element(d)), src_row),
                  pl.BlockSpec((pl.Element(1), pl.Element(LANES), pl.Element(d)),
                               lambda j, b: (j, STATE_W + RANK_COLS - LANES, 0))],
        out_specs=[pl.BlockSpec((None, d, tn), lambda j, b: (j, 0, b)),
                   pl.BlockSpec((None, d, LANES), lambda j, b: (j, 0, 0))],
        out_shape=[jax.ShapeDtypeStruct((ne, d, n - RANK_COLS), BF16),
                   jax.ShapeDtypeStruct((ne, d, LANES), BF16)],
        compiler_params=_params("parallel", "arbitrary"),
        name="even_w_prep",
    )(wt, wt)


def _gate_weights(w_gf, b_gf, w_gb, b_gb):
    zr = jnp.zeros((GLA_GATE_RANK, GLA_KEY_W), BF16)
    tail = jnp.zeros((LANES - 3 * RANK_COLS, GLA_KEY_W), BF16)
    f_hi, f_lo = _hi_lo(w_gf * LOG2_E)
    b_hi, b_lo = _hi_lo(w_gb * LOG2_E)
    wg3 = jnp.stack([jnp.concatenate([f_hi, zr, f_hi, zr, f_lo, zr, tail], axis=0),
                     jnp.concatenate([zr, b_hi, zr, b_hi, zr, b_lo, tail], axis=0)])
    bg = jnp.stack([b_gf, b_gb])[:, None, :] * LOG2_E
    return wg3, bg


def kernel(x, c, ctx, c_ctx, w_ada, b_ada, ln_g, ln_b, w_in_e, w_gate_f, b_gate_f, w_gate_b,
           b_gate_b, gla_norm_w, conv_w, w_out_e, w_in_o, w_pool, pool_scale, w_out_o):
    bn, L, d = x.shape
    lc = ctx.shape[1]
    cond = jnp.concatenate([c, c_ctx[None, :], jnp.zeros((COND_ROWS - bn - 1, d), F32)], axis=0)
    mod = _modulation(cond, w_ada, b_ada)
    s_zero = jnp.zeros((2, bn, GLA_HEADS, GLA_DV, GLA_DK), F32)
    w_main, w_r = _even_w_prep(w_in_e)
    w_out_e, w_in_o, w_out_o, w_pool = (
        w.astype(BF16) for w in (w_out_e, w_in_o, w_out_o, w_pool))
    ctx_s = ctx
    for i in range(DEPTH):
        j = i // 2
        ctx_needed = any(l % 2 == 0 for l in range(i + 1, DEPTH))
        sh, sc, gt = (mod[i, :bn, n * d:(n + 1) * d][:, None, :] for n in range(3))
        sh_c, sc_c, gt_c = (jnp.broadcast_to(mod[i, bn, n * d:(n + 1) * d], (bn, 1, d))
                            for n in range(3))
        lg, lb = ln_g[i][None, :], ln_b[i][None, :]
        if i % 2 == 0:
            wg3, bg = _gate_weights(w_gate_f[j], b_gate_f[j], w_gate_b[j], b_gate_b[j])
            norm_w = gla_norm_w[j][None, :]
            v_c, fac_c, scl_c, gb_c, ya_c = _even_in(ctx_s, sh_c, sc_c, w_main, w_r, j, conv_w[j],
                                                     wg3, bg, lc)
            o_c, s_ctx = _gla(v_c, fac_c, scl_c, s_zero)
            if ctx_needed:
                ctx_s = _even_out(o_c, gb_c, ya_c, norm_w, w_out_e, j, ctx_s, gt_c, lg, lb)
            v, fac, scl, gb, ya = _even_in(x, sh, sc, w_main, w_r, j, conv_w[j], wg3, bg, GRID_W)
            o, _ = _gla(v, fac, scl, s_ctx)
            x = _even_out(o, gb, ya, norm_w, w_out_e, j, x, gt, lg, lb)
        else:
            ps = pool_scale[j][None, :]
            if ctx_needed:
                m_c = _odd_mix(ctx_s, sh_c, sc_c, w_in_o, w_pool, j, ps, 1)
                ctx_s = _out_ln(m_c, w_out_o, j, ctx_s, gt_c, lg, lb)
            m = _odd_mix(x, sh, sc, w_in_o, w_pool, j, ps, GRID_W)
            x = _out_ln(m, w_out_o, j, x, gt, lg, lb)
    return x
```

```python
import functools

import jax
import jax.numpy as jnp
from jax import lax
from jax.experimental import pallas as pl
from jax.experimental.pallas import tpu as pltpu

D_MODEL = 2048
DEPTH = 4
GRID_W = 64

GLA_HEADS = 4
GLA_DK = 256
GLA_DV = 256
GLA_KEY_W = GLA_HEADS * GLA_DK
GLA_VAL_W = GLA_HEADS * GLA_DV
GLA_GATE_RANK = 16
GLA_TAU = 16.0
CONV_W = D_MODEL // 2
POOL_W = D_MODEL
POOL_WINDOWS = (2, 4, 8, 16)
POOL_GROUP = POOL_W // len(POOL_WINDOWS)
POOL_HALO_ROWS = max(POOL_WINDOWS) // 2
LN_EPS = 1e-5
ALPHA = (2.0 * DEPTH) ** 0.25

LANES = 128
COND_ROWS = 8
VMEM_LIMIT = 60000 * 1024
GLA_CHUNK = 256
GLA_BLOCK_CHUNKS = 8
SUB_TILE = 256
EVEN_OUT_TILE = 512
OUT_LN_TILE = 1024
LN_ROWS = 8
NORM_ROWS = 16
POOL_TILE = 512
RANK_COLS = 2 * GLA_GATE_RANK
STATE_W = GLA_KEY_W + GLA_VAL_W
MAIN_BLOCKS = ("k", "v", "q", "gate_b", "a_b", "a_c", "a_x", "gate_a")
MOD_COLS = 1536
PREP_COLS = 1024
LOG2_E = 1.4426950408889634

F32 = jnp.float32
BF16 = jnp.bfloat16

_NT = (((1,), (1,)), ((), ()))
_TN = (((0,), (0,)), ((), ()))


def _silu(t):
    return t / (1.0 + jnp.exp(-t))


def _resident(shape):
    return pl.BlockSpec(shape, lambda *_: (0,) * len(shape), pipeline_mode=pl.Buffered(1))


def _layer_resident(stacked_shape, layer):
    nd = len(stacked_shape)
    return pl.BlockSpec((None,) + tuple(stacked_shape[1:]), lambda *_: (layer,) + (0,) * (nd - 1),
                        pipeline_mode=pl.Buffered(1))


def _params(*sem):
    return pltpu.CompilerParams(dimension_semantics=sem, vmem_limit_bytes=VMEM_LIMIT)


def _modulation_kernel(c_ref, w_ref, b_ref, o_ref):
    s = _silu(c_ref[...]).astype(BF16)
    o_ref[...] = jnp.dot(s, w_ref[...].astype(BF16), preferred_element_type=F32) + b_ref[...]


def _modulation(cond, w_ada, b_ada):
    depth, d, n = w_ada.shape
    tn = MOD_COLS
    return pl.pallas_call(
        _modulation_kernel,
        grid=(depth, n // tn),
        in_specs=[
            pl.BlockSpec((COND_ROWS, d), lambda i, j: (0, 0)),
            pl.BlockSpec((None, d, tn), lambda i, j: (i, 0, j)),
            pl.BlockSpec((None, 1, tn), lambda i, j: (i, 0, j)),
        ],
        out_specs=pl.BlockSpec((None, COND_ROWS, tn), lambda i, j: (i, 0, j)),
        out_shape=jax.ShapeDtypeStruct((depth, COND_ROWS, n), F32),
        compiler_params=_params("parallel", "parallel"),
        name="modulation",
    )(cond, w_ada, b_ada.reshape(depth, 1, n))


def _gate_log2(r3, wg, bg):
    z2 = jnp.dot(r3, wg, preferred_element_type=F32) + bg
    ls2 = jnp.minimum(z2, 0.0) - jnp.log2(1.0 + jnp.exp2(-jnp.abs(z2)))
    ls_hi = ls2.astype(BF16)
    ls_lo = (ls2 - ls_hi.astype(F32)).astype(BF16)
    return jnp.concatenate([ls_hi, ls_lo], axis=0)


def _running_decay(ls, fwd):
    chunk = ls.shape[0] // 2
    row = lax.broadcasted_iota(jnp.int32, (chunk, chunk), 0)
    col = lax.broadcasted_iota(jnp.int32, (chunk, chunk), 1)
    tri_b = jnp.where((row >= col) if fwd else (row <= col), 1.0 / GLA_TAU, 0.0).astype(BF16)
    return jnp.dot(jnp.concatenate([tri_b, tri_b], axis=1), ls, preferred_element_type=F32)


def _store_decay_factors(q, k, g, fwd, fac_ref, scl_ref):
    chunk = q.shape[0]
    p = g[chunk // 2:chunk // 2 + 1, :]
    g_tot = g[chunk - 1:chunk, :] if fwd else g[0:1, :]
    fac_ref[:, 0:GLA_KEY_W] = q * jnp.exp2(g - p).astype(BF16)
    fac_ref[:, GLA_KEY_W:] = k * jnp.exp2(p - g).astype(BF16)
    pad = jnp.zeros((COND_ROWS - 3, GLA_KEY_W), F32)
    scl_ref[...] = jnp.concatenate(
        [jnp.exp2(p), jnp.exp2(g_tot), jnp.exp2(g_tot - p), pad], axis=0)


def _even_in_kernel(x_ref, sh_ref, sc_ref, w_ref, wr_ref, cw_ref, wg_ref, bg_ref,
                    v_ref, fac_ref, scl_ref, gb_ref, ya_ref, *, row_w):
    tm = x_ref.shape[0]
    u = (x_ref[...] * (1.0 + sc_ref[...]) + sh_ref[...]).astype(BF16)

    def proj(name):
        col = MAIN_BLOCKS.index(name) * GLA_KEY_W
        return jnp.dot(u, w_ref[:, col:col + GLA_KEY_W], preferred_element_type=F32)

    r = jnp.dot(u, wr_ref[...], preferred_element_type=F32)
    r_hi = r.astype(BF16)
    r_lo = (r - r_hi.astype(F32)).astype(BF16)
    lane = lax.broadcasted_iota(jnp.int32, r.shape, 1)
    r3 = jnp.where((lane >= RANK_COLS) & (lane < 2 * RANK_COLS), r_lo, r_hi)

    k = proj("k").astype(BF16)
    ls = [_gate_log2(r3, wg_ref[d], bg_ref[d]) for d in range(2)]
    gate_a = _silu(proj("gate_a"))
    q = (proj("q") * GLA_DK ** -0.5).astype(BF16)
    g = [_running_decay(ls[d], d == 0) for d in range(2)]
    gb_ref[...] = _silu(proj("gate_b")).astype(BF16)
    for d in range(2):
        _store_decay_factors(q, k, g[d], d == 0, fac_ref.at[d], scl_ref.at[d])

    p = proj("a_c") * proj("a_x")
    t = lax.broadcasted_iota(jnp.int32, (tm, 1), 0) % row_w
    prev = jnp.where(t == 0, 0.0, pltpu.roll(p, 1, axis=0))
    nxt = jnp.where(t == row_w - 1, 0.0, pltpu.roll(p, tm - 1, axis=0))
    conv = cw_ref[0:1, :] * prev + cw_ref[1:2, :] * p + cw_ref[2:3, :] * nxt
    ya_ref[...] = (proj("a_b") * conv * gate_a).astype(BF16)
    v_ref[...] = proj("v").astype(BF16)


def _even_in(x, sh, sc, w_main, w_r, layer, conv_w, wg3, bg, row_w):
    bn, L, d = x.shape
    tm = min(GLA_CHUNK, L)
    assert L % tm == 0 and tm % row_w == 0
    tile = lambda n: pl.BlockSpec((None, tm, n), lambda b, i: (b, i, 0))
    vec = pl.BlockSpec((None, 1, d), lambda b, i: (b, 0, 0))
    return pl.pallas_call(
        functools.partial(_even_in_kernel, row_w=row_w),
        grid=(bn, L // tm),
        in_specs=[tile(d), vec, vec, _layer_resident(w_main.shape, layer),
                  _layer_resident(w_r.shape, layer),
                  _resident(conv_w.shape), _resident(wg3.shape), _resident(bg.shape)],
        out_specs=[tile(GLA_VAL_W),
                   pl.BlockSpec((2, None, tm, 2 * GLA_KEY_W), lambda b, i: (0, b, i, 0)),
                   pl.BlockSpec((2, None, None, COND_ROWS, GLA_KEY_W), lambda b, i: (0, b, i, 0, 0)),
                   tile(GLA_VAL_W), tile(CONV_W)],
        out_shape=[jax.ShapeDtypeStruct((bn, L, GLA_VAL_W), BF16),
                   jax.ShapeDtypeStruct((2, bn, L, 2 * GLA_KEY_W), BF16),
                   jax.ShapeDtypeStruct((2, bn, L // tm, COND_ROWS, GLA_KEY_W), F32),
                   jax.ShapeDtypeStruct((bn, L, GLA_VAL_W), BF16),
                   jax.ShapeDtypeStruct((bn, L, CONV_W), BF16)],
        compiler_params=_params("parallel", "parallel"),
        name="even_in",
    )(x, sh, sc, w_main, w_r, conv_w, wg3, bg)


def _gla_kernel(fac_ref, scl_ref, v_ref, s0_ref, o_ref, sfin_ref, st_scr, *, chunk):
    d = pl.program_id(0)
    c = pl.program_id(2)
    n_sub = v_ref.shape[0] // chunk

    @pl.when(c == 0)
    def _():
        st_scr[...] = s0_ref[...]

    row = lax.broadcasted_iota(jnp.int32, (chunk, chunk), 0)
    col = lax.broadcasted_iota(jnp.int32, (chunk, chunk), 1)
    tri = jnp.where(d == 0, row - col, col - row) >= 0

    for s in range(n_sub):
        pos = s + d * (n_sub - 1 - 2 * s)
        rows = pl.ds(pl.multiple_of(pos * chunk, chunk), chunk)
        e_p = scl_ref[pos, 0:1, :]
        e_tot = scl_ref[pos, 1:2, :]
        e_kd = scl_ref[pos, 2:3, :].astype(BF16)
        for h in range(GLA_HEADS):
            hs = slice(h * GLA_DK, (h + 1) * GLA_DK)
            qg = fac_ref[rows, hs]
            kg = fac_ref[rows, GLA_KEY_W + h * GLA_DK:GLA_KEY_W + (h + 1) * GLA_DK]
            kd = kg * e_kd[:, hs]
            v = v_ref[rows, hs]
            a = lax.dot_general(qg, kg, _NT, preferred_element_type=F32)
            a = jnp.where(tri, a, 0.0).astype(BF16)
            st = st_scr[h]
            o = jnp.dot(a, v, preferred_element_type=F32)
            o += lax.dot_general(qg, (st * e_p[:, hs]).astype(BF16), _NT,
                                 preferred_element_type=F32)
            o_ref[rows, hs] = o.astype(BF16)
            st_scr[h] = st * e_tot[:, hs] + lax.dot_general(v, kd, _TN,
                                                            preferred_element_type=F32)

    @pl.when(c == pl.num_programs(2) - 1)
    def _():
        sfin_ref[...] = st_scr[...]


def _gla(v, fac, scl, s0):
    bn, L, _ = v.shape
    chunk = min(GLA_CHUNK, L)
    blk = min(GLA_BLOCK_CHUNKS * chunk, L)
    nc = L // blk
    assert L % blk == 0 and blk % chunk == 0

    def cidx(d, c):
        return c + d * (nc - 1 - 2 * c)

    state = pl.BlockSpec((None, None, GLA_HEADS, GLA_DV, GLA_DK), lambda d, b, c: (d, b, 0, 0, 0))
    return pl.pallas_call(
        functools.partial(_gla_kernel, chunk=chunk),
        grid=(2, bn, nc),
        in_specs=[pl.BlockSpec((None, None, blk, 2 * GLA_KEY_W),
                               lambda d, b, c: (d, b, cidx(d, c), 0)),
                  pl.BlockSpec((None, None, blk // chunk, COND_ROWS, GLA_KEY_W),
                               lambda d, b, c: (d, b, cidx(d, c), 0, 0)),
                  pl.BlockSpec((None, blk, GLA_VAL_W), lambda d, b, c: (b, cidx(d, c), 0)),
                  state],
        out_specs=[pl.BlockSpec((None, None, blk, GLA_VAL_W), lambda d, b, c: (d, b, cidx(d, c), 0)),
                   state],
        out_shape=[jax.ShapeDtypeStruct((2, bn, L, GLA_VAL_W), BF16),
                   jax.ShapeDtypeStruct((2, bn, GLA_HEADS, GLA_DV, GLA_DK), F32)],
        scratch_shapes=[pltpu.VMEM((GLA_HEADS, GLA_DV, GLA_DK), F32)],
        compiler_params=_params("parallel", "parallel", "arbitrary"),
        name="gla",
    )(fac, scl, v, s0)


def _residual_ln_rows(x_ref, out_ref, row0, y, gts, g, b):
    for r in range(0, y.shape[0], LN_ROWS):
        rows = pl.ds(row0 + r, LN_ROWS)
        w = x_ref[rows, :] + gts * y[r:r + LN_ROWS, :]
        mu = jnp.mean(w, axis=-1, keepdims=True)
        xc = w - mu
        var = jnp.mean(xc * xc, axis=-1, keepdims=True)
        out_ref[rows, :] = xc * lax.rsqrt(var + LN_EPS / ALPHA ** 2) * g + b


def _gla_branch_rows(of_ref, ob_ref, gb_ref, nw, row0, yb_ref):
    for r in range(0, SUB_TILE, NORM_ROWS):
        rows = pl.ds(row0 + r, NORM_ROWS)
        o = of_ref[rows, :].astype(F32) + ob_ref[rows, :].astype(F32)
        heads = []
        for h in range(GLA_HEADS):
            oh = o[:, h * GLA_DV:(h + 1) * GLA_DV]
            ms = jnp.mean(oh * oh, axis=-1, keepdims=True)
            heads.append(oh * lax.rsqrt(ms + LN_EPS) * nw)
        yb = jnp.concatenate(heads, axis=-1) * gb_ref[rows, :].astype(F32)
        yb_ref[r:r + NORM_ROWS, :] = yb.astype(BF16)


def _even_out_kernel(of_ref, ob_ref, gb_ref, ya_ref, nw_ref, w_ref, x_ref, gt_ref, lg_ref, lb_ref,
                     out_ref, yb_scr):
    n_sub = x_ref.shape[0] // SUB_TILE
    sub = lambda s: pl.ds(s * SUB_TILE, SUB_TILE)
    gts = gt_ref[...] * (1.0 / ALPHA)
    nw, lg, lb = nw_ref[...], lg_ref[...], lb_ref[...]
    _gla_branch_rows(of_ref, ob_ref, gb_ref, nw, 0, yb_scr.at[0])
    y = None
    for s in range(n_sub + 1):
        y_prev = y
        if s < n_sub:
            y = jnp.dot(yb_scr[s % 2], w_ref[0:GLA_VAL_W, :], preferred_element_type=F32)
            y += jnp.dot(ya_ref[sub(s), :], w_ref[GLA_VAL_W:, :], preferred_element_type=F32)
        if s + 1 < n_sub:
            _gla_branch_rows(of_ref, ob_ref, gb_ref, nw, (s + 1) * SUB_TILE, yb_scr.at[(s + 1) % 2])
        if s > 0:
            _residual_ln_rows(x_ref, out_ref, (s - 1) * SUB_TILE, y_prev, gts, lg, lb)


def _even_out(o, gb, ya, norm_w, w_out, layer, x, gt, ln_g, ln_b):
    bn, L, d = x.shape
    tm = min(EVEN_OUT_TILE, L)
    assert L % tm == 0 and tm % SUB_TILE == 0
    tile = lambda n: pl.BlockSpec((None, tm, n), lambda b, i: (b, i, 0))
    odir = lambda k: pl.BlockSpec((None, None, tm, GLA_VAL_W), lambda b, i: (k, b, i, 0))
    return pl.pallas_call(
        _even_out_kernel,
        grid=(bn, L // tm),
        in_specs=[odir(0), odir(1), tile(GLA_VAL_W), tile(CONV_W), _resident(norm_w.shape),
                  _layer_resident(w_out.shape, layer), tile(d),
                  pl.BlockSpec((None, 1, d), lambda b, i: (b, 0, 0)),
                  _resident(ln_g.shape), _resident(ln_b.shape)],
        out_specs=tile(d),
        out_shape=jax.ShapeDtypeStruct((bn, L, d), F32),
        scratch_shapes=[pltpu.VMEM((2, SUB_TILE, GLA_VAL_W), BF16)],
        compiler_params=_params("parallel", "parallel"),
        name="even_out",
    )(o, o, gb, ya, norm_w, w_out, x, gt, ln_g, ln_b)


def _out_ln_kernel(m_ref, w_ref, x_ref, gt_ref, lg_ref, lb_ref, out_ref):
    n_sub = x_ref.shape[0] // SUB_TILE
    sub = lambda s: pl.ds(s * SUB_TILE, SUB_TILE)
    gts = gt_ref[...] * (1.0 / ALPHA)
    lg, lb = lg_ref[...], lb_ref[...]
    y = None
    for s in range(n_sub + 1):
        y_prev = y
        if s < n_sub:
            y = jnp.dot(m_ref[sub(s), :], w_ref[...], preferred_element_type=F32)
        if s > 0:
            _residual_ln_rows(x_ref, out_ref, (s - 1) * SUB_TILE, y_prev, gts, lg, lb)


def _out_ln(m, w_out, layer, x, gt, ln_g, ln_b):
    bn, L, d = x.shape
    tm = min(OUT_LN_TILE, L)
    assert L % tm == 0 and tm % SUB_TILE == 0
    tile = lambda n: pl.BlockSpec((None, tm, n), lambda b, i: (b, i, 0))
    return pl.pallas_call(
        _out_ln_kernel,
        grid=(bn, L // tm),
        in_specs=[tile(m.shape[-1]), _layer_resident(w_out.shape, layer), tile(d),
                  pl.BlockSpec((None, 1, d), lambda b, i: (b, 0, 0)),
                  _resident(ln_g.shape), _resident(ln_b.shape)],
        out_specs=tile(d),
        out_shape=jax.ShapeDtypeStruct((bn, L, d), F32),
        compiler_params=_params("parallel", "parallel"),
        name="out_ln",
    )(m, w_out, x, gt, ln_g, ln_b)


def _odd_mix_kernel(xp_ref, xc_ref, xn_ref, sh_ref, sc_ref, win_ref, wpool_ref, ps_ref, m_ref,
                    *, row_stride, n_rows):
    i = pl.program_id(1)
    tm = xc_ref.shape[0]
    halo = POOL_HALO_ROWS * row_stride
    xc = xc_ref[...]
    scale1 = 1.0 + sc_ref[...]
    u = (xc * scale1 + sh_ref[...]).astype(BF16)
    sgate = _silu(jnp.dot(u, win_ref[:, POOL_W:], preferred_element_type=F32)).astype(BF16)

    before = jnp.where(i > 0, xp_ref[tm - halo:, :], 0.0)
    after = jnp.where(i < pl.num_programs(1) - 1, xn_ref[:halo, :], 0.0)
    xe = jnp.concatenate([before, xc, after], axis=0)
    grid_row = (i * tm + lax.broadcasted_iota(jnp.int32, (tm, 1), 0)) // row_stride

    def window_diff(s, w):
        half = w // 2
        s = s[:s.shape[0] - half * row_stride, :] + s[half * row_stride:, :]
        start = halo - half * row_stride
        cnt = jnp.minimum(grid_row + half, n_rows) - jnp.maximum(grid_row - half, 0)
        mean = s[start:start + tm, :] / cnt.astype(F32)
        return s, ((mean - xc) * scale1).astype(BF16)

    s, diff = window_diff(xe, POOL_WINDOWS[0])
    for g in range(len(POOL_WINDOWS)):
        cur = diff
        if g + 1 < len(POOL_WINDOWS):
            s, diff = window_diff(s, POOL_WINDOWS[g + 1])
        cols = slice(g * POOL_GROUP, (g + 1) * POOL_GROUP)
        z = jnp.dot(cur, win_ref[:, cols], preferred_element_type=F32)
        z = jnp.dot(z.astype(BF16), wpool_ref[g], preferred_element_type=F32)
        m_ref[:, cols] = (z * ps_ref[:, cols] * sgate[:, cols].astype(F32)).astype(BF16)


def _odd_mix(x, sh, sc, w_in, w_pool, layer, pool_scale, row_stride):
    bn, L, d = x.shape
    n_rows = L // row_stride
    tm = min(POOL_TILE, L)
    nt = L // tm
    assert L % tm == 0 and tm % row_stride == 0 and POOL_HALO_ROWS * row_stride <= tm
    vec = pl.BlockSpec((None, 1, d), lambda b, i: (b, 0, 0))
    return pl.pallas_call(
        functools.partial(_odd_mix_kernel, row_stride=row_stride, n_rows=n_rows),
        grid=(bn, nt),
        in_specs=[pl.BlockSpec((None, tm, d), lambda b, i: (b, jnp.maximum(i - 1, 0), 0)),
                  pl.BlockSpec((None, tm, d), lambda b, i: (b, i, 0)),
                  pl.BlockSpec((None, tm, d), lambda b, i: (b, jnp.minimum(i + 1, nt - 1), 0)),
                  vec, vec, _layer_resident(w_in.shape, layer), _layer_resident(w_pool.shape, layer),
                  _resident(pool_scale.shape)],
        out_specs=pl.BlockSpec((None, tm, POOL_W), lambda b, i: (b, i, 0)),
        out_shape=jax.ShapeDtypeStruct((bn, L, POOL_W), BF16),
        compiler_params=_params("parallel", "parallel"),
        name="odd_mix",
    )(x, x, x, sh, sc, w_in, w_pool, pool_scale)


def _hi_lo(w):
    hi = w.astype(BF16)
    return hi, (w - hi.astype(F32)).astype(BF16)


def _even_w_prep_kernel(wt_ref, rank_ref, main_ref, r_ref):
    main_ref[...] = wt_ref[0].T.astype(BF16)
    rank = rank_ref[0].T[:, LANES - RANK_COLS:].astype(BF16)
    pad = jnp.zeros((rank.shape[0], LANES - 3 * RANK_COLS), BF16)
    r_ref[...] = jnp.concatenate([rank, rank, rank, pad], axis=1)


def _even_w_prep(w_in_e):
    ne, d, n = w_in_e.shape
    tn = PREP_COLS
    assert STATE_W % tn == 0 and (n - RANK_COLS) % tn == 0
    wt = jnp.transpose(w_in_e, (0, 2, 1))

    def src_row(j, b):
        row = b * tn + jnp.where(b * tn >= STATE_W, RANK_COLS, 0)
        return (j, pl.multiple_of(row, RANK_COLS), 0)

    return pl.pallas_call(
        _even_w_prep_kernel,
        grid=(ne, (n - RANK_COLS) // tn),
        in_specs=[pl.BlockSpec((pl.Element(1), pl.Element(tn), pl.Element(d)), src_row),
                  pl.BlockSpec((pl.Element(1), pl.Element(LANES), pl.Element(d)),
                               lambda j, b: (j, STATE_W + RANK_COLS - LANES, 0))],
        out_specs=[pl.BlockSpec((None, d, tn), lambda j, b: (j, 0, b)),
                   pl.BlockSpec((None, d, LANES), lambda j, b: (j, 0, 0))],
        out_shape=[jax.ShapeDtypeStruct((ne, d, n - RANK_COLS), BF16),
                   jax.ShapeDtypeStruct((ne, d, LANES), BF16)],
        compiler_params=_params("parallel", "arbitrary"),
        name="even_w_prep",
    )(wt, wt)


def _gate_weights(w_gf, b_gf, w_gb, b_gb):
    zr = jnp.zeros((GLA_GATE_RANK, GLA_KEY_W), BF16)
    tail = jnp.zeros((LANES - 3 * RANK_COLS, GLA_KEY_W), BF16)
    f_hi, f_lo = _hi_lo(w_gf * LOG2_E)
    b_hi, b_lo = _hi_lo(w_gb * LOG2_E)
    wg3 = jnp.stack([jnp.concatenate([f_hi, zr, f_hi, zr, f_lo, zr, tail], axis=0),
                     jnp.concatenate([zr, b_hi, zr, b_hi, zr, b_lo, tail], axis=0)])
    bg = jnp.stack([b_gf, b_gb])[:, None, :] * LOG2_E
    return wg3, bg


def kernel(x, c, ctx, c_ctx, w_ada, b_ada, ln_g, ln_b, w_in_e, w_gate_f, b_gate_f, w_gate_b,
           b_gate_b, gla_norm_w, conv_w, w_out_e, w_in_o, w_pool, pool_scale, w_out_o):
    bn, L, d = x.shape
    lc = ctx.shape[1]
    cond = jnp.concatenate([c, c_ctx[None, :], jnp.zeros((COND_ROWS - bn - 1, d), F32)], axis=0)
    mod = _modulation(cond, w_ada, b_ada)
    s_zero = jnp.zeros((2, bn, GLA_HEADS, GLA_DV, GLA_DK), F32)
    w_main, w_r = _even_w_prep(w_in_e)
    w_out_e, w_in_o, w_out_o, w_pool = (
        w.astype(BF16) for w in (w_out_e, w_in_o, w_out_o, w_pool))
    ctx_s = ctx
    for i in range(DEPTH):
        j = i // 2
        ctx_needed = any(l % 2 == 0 for l in range(i + 1, DEPTH))
        sh, sc, gt = (mod[i, :bn, n * d:(n + 1) * d][:, None, :] for n in range(3))
        sh_c, sc_c, gt_c = (jnp.broadcast_to(mod[i, bn, n * d:(n + 1) * d], (bn, 1, d))
                            for n in range(3))
        lg, lb = ln_g[i][None, :], ln_b[i][None, :]
        if i % 2 == 0:
            wg3, bg = _gate_weights(w_gate_f[j], b_gate_f[j], w_gate_b[j], b_gate_b[j])
            norm_w = gla_norm_w[j][None, :]
            v_c, fac_c, scl_c, gb_c, ya_c = _even_in(ctx_s, sh_c, sc_c, w_main, w_r, j, conv_w[j],
                                                     wg3, bg, lc)
            o_c, s_ctx = _gla(v_c, fac_c, scl_c, s_zero)
            if ctx_needed:
                ctx_s = _even_out(o_c, gb_c, ya_c, norm_w, w_out_e, j, ctx_s, gt_c, lg, lb)
            v, fac, scl, gb, ya = _even_in(x, sh, sc, w_main, w_r, j, conv_w[j], wg3, bg, GRID_W)
            o, _ = _gla(v, fac, scl, s_ctx)
            x = _even_out(o, gb, ya, norm_w, w_out_e, j, x, gt, lg, lb)
        else:
            ps = pool_scale[j][None, :]
            if ctx_needed:
                m_c = _odd_mix(ctx_s, sh_c, sc_c, w_in_o, w_pool, j, ps, 1)
                ctx_s = _out_ln(m_c, w_out_o, j, ctx_s, gt_c, lg, lb)
            m = _odd_mix(x, sh, sc, w_in_o, w_pool, j, ps, GRID_W)
            x = _out_ln(m, w_out_o, j, x, gt, lg, lb)
    return x
```

```python
import functools

import jax
import jax.numpy as jnp
from jax import lax
from jax.experimental import pallas as pl
from jax.experimental.pallas import tpu as pltpu

D_MODEL = 2048
DEPTH = 4
GRID_W = 64

GLA_HEADS = 4
GLA_DK = 256
GLA_DV = 256
GLA_KEY_W = GLA_HEADS * GLA_DK
GLA_VAL_W = GLA_HEADS * GLA_DV
GLA_GATE_RANK = 16
GLA_TAU = 16.0
CONV_W = D_MODEL // 2
POOL_W = D_MODEL
POOL_WINDOWS = (2, 4, 8, 16)
POOL_GROUP = POOL_W // len(POOL_WINDOWS)
POOL_HALO_ROWS = max(POOL_WINDOWS) // 2
LN_EPS = 1e-5
ALPHA = (2.0 * DEPTH) ** 0.25

LANES = 128
COND_ROWS = 8
VMEM_LIMIT = 60000 * 1024
GLA_CHUNK = 256
GLA_BLOCK_CHUNKS = 8
SUB_TILE = 256
EVEN_OUT_TILE = 512
OUT_LN_TILE = 1024
LN_ROWS = 8
NORM_ROWS = 16
POOL_TILE = 512
RANK_COLS = 2 * GLA_GATE_RANK
STATE_W = GLA_KEY_W + GLA_VAL_W
MAIN_BLOCKS = ("k", "v", "q", "gate_b", "a_b", "a_c", "a_x", "gate_a")
MOD_COLS = 1536
PREP_COLS = 1024
LOG2_E = 1.4426950408889634

F32 = jnp.float32
BF16 = jnp.bfloat16

_NT = (((1,), (1,)), ((), ()))
_TN = (((0,), (0,)), ((), ()))


def _silu(t):
    return t / (1.0 + jnp.exp(-t))


def _resident(shape):
    return pl.BlockSpec(shape, lambda *_: (0,) * len(shape), pipeline_mode=pl.Buffered(1))


def _layer_resident(stacked_shape, layer):
    nd = len(stacked_shape)
    return pl.BlockSpec((None,) + tuple(stacked_shape[1:]), lambda *_: (layer,) + (0,) * (nd - 1),
                        pipeline_mode=pl.Buffered(1))


def _params(*sem):
    return pltpu.CompilerParams(dimension_semantics=sem, vmem_limit_bytes=VMEM_LIMIT)


def _modulation_kernel(c_ref, w_ref, b_ref, o_ref):
    s = _silu(c_ref[...]).astype(BF16)
    o_ref[...] = jnp.dot(s, w_ref[...].astype(BF16), preferred_element_type=F32) + b_ref[...]


def _modulation(cond, w_ada, b_ada):
    depth, d, n = w_ada.shape
    tn = MOD_COLS
    return pl.pallas_call(
        _modulation_kernel,
        grid=(depth, n // tn),
        in_specs=[
            pl.BlockSpec((COND_ROWS, d), lambda i, j: (0, 0)),
            pl.BlockSpec((None, d, tn), lambda i, j: (i, 0, j)),
            pl.BlockSpec((None, 1, tn), lambda i, j: (i, 0, j)),
        ],
        out_specs=pl.BlockSpec((None, COND_ROWS, tn), lambda i, j: (i, 0, j)),
        out_shape=jax.ShapeDtypeStruct((depth, COND_ROWS, n), F32),
        compiler_params=_params("parallel", "parallel"),
        name="modulation",
    )(cond, w_ada, b_ada.reshape(depth, 1, n))


def _gate_log2(r3, wg, bg):
    z2 = jnp.dot(r3, wg, preferred_element_type=F32) + bg
    ls2 = jnp.minimum(z2, 0.0) - jnp.log2(1.0 + jnp.exp2(-jnp.abs(z2)))
    ls_hi = ls2.astype(BF16)
    ls_lo = (ls2 - ls_hi.astype(F32)).astype(BF16)
    return jnp.concatenate([ls_hi, ls_lo], axis=0)


def _running_decay(ls, fwd):
    chunk = ls.shape[0] // 2
    row = lax.broadcasted_iota(jnp.int32, (chunk, chunk), 0)
    col = lax.broadcasted_iota(jnp.int32, (chunk, chunk), 1)
    tri_b = jnp.where((row >= col) if fwd else (row <= col), 1.0 / GLA_TAU, 0.0).astype(BF16)
    return jnp.dot(jnp.concatenate([tri_b, tri_b], axis=1), ls, preferred_element_type=F32)


def _store_decay_factors(q, k, g, fwd, fac_ref, scl_ref):
    chunk = q.shape[0]
    p = g[chunk // 2:chunk // 2 + 1, :]
    g_tot = g[chunk - 1:chunk, :] if fwd else g[0:1, :]
    fac_ref[:, 0:GLA_KEY_W] = q * jnp.exp2(g - p).astype(BF16)
    fac_ref[:, GLA_KEY_W:] = k * jnp.exp2(p - g).astype(BF16)
    pad = jnp.zeros((COND_ROWS - 3, GLA_KEY_W), F32)
    scl_ref[...] = jnp.concatenate(
        [jnp.exp2(p), jnp.exp2(g_tot), jnp.exp2(g_tot - p), pad], axis=0)


def _gla_heads(qg_all, kg_all, kd_all, e_p, e_tot, v_all, tri, st_scr, o_store):
    for h in range(GLA_HEADS):
        hs = slice(h * GLA_DK, (h + 1) * GLA_DK)
        qg, v = qg_all[:, hs], v_all[:, hs]
        a = lax.dot_general(qg, kg_all[:, hs], _NT, preferred_element_type=F32)
        a = jnp.where(tri, a, 0.0).astype(BF16)
        st = st_scr[h]
        o = jnp.dot(a, v, preferred_element_type=F32)
        o += lax.dot_general(qg, (st * e_p[:, hs]).astype(BF16), _NT, preferred_element_type=F32)
        o_store(hs, o.astype(BF16))
        st_scr[h] = st * e_tot[:, hs] + lax.dot_general(v, kd_all[:, hs], _TN,
                                                        preferred_element_type=F32)


def _even_in_kernel(x_ref, sh_ref, sc_ref, w_ref, wr_ref, cw_ref, wg_ref, bg_ref, s0_ref,
                    v_ref, fac_ref, scl_ref, gb_ref, ya_ref, of_ref, sfin_ref, st_scr, *, row_w):
    i = pl.program_id(1)
    tm = x_ref.shape[0]

    @pl.when(i == 0)
    def _():
        st_scr[...] = s0_ref[...]

    u = (x_ref[...] * (1.0 + sc_ref[...]) + sh_ref[...]).astype(BF16)

    def proj(name):
        col = MAIN_BLOCKS.index(name) * GLA_KEY_W
        return jnp.dot(u, w_ref[:, col:col + GLA_KEY_W], preferred_element_type=F32)

    r = jnp.dot(u, wr_ref[...], preferred_element_type=F32)
    r_hi = r.astype(BF16)
    r_lo = (r - r_hi.astype(F32)).astype(BF16)
    lane = lax.broadcasted_iota(jnp.int32, r.shape, 1)
    r3 = jnp.where((lane >= RANK_COLS) & (lane < 2 * RANK_COLS), r_lo, r_hi)

    k = proj("k").astype(BF16)
    ls = [_gate_log2(r3, wg_ref[d], bg_ref[d]) for d in range(2)]
    v = proj("v").astype(BF16)
    v_ref[...] = v
    q = (proj("q") * GLA_DK ** -0.5).astype(BF16)
    g = [_running_decay(ls[d], d == 0) for d in range(2)]
    gb_ref[...] = _silu(proj("gate_b")).astype(BF16)
    _store_decay_factors(q, k, g[1], False, fac_ref, scl_ref)

    gf = g[0]
    p = gf[tm // 2:tm // 2 + 1, :]
    g_tot = gf[tm - 1:tm, :]
    qg = q * jnp.exp2(gf - p).astype(BF16)
    kg = k * jnp.exp2(p - gf).astype(BF16)
    kd = kg * jnp.exp2(g_tot - p).astype(BF16)
    row = lax.broadcasted_iota(jnp.int32, (tm, tm), 0)
    col = lax.broadcasted_iota(jnp.int32, (tm, tm), 1)

    def store_o(hs, o):
        of_ref[:, hs] = o

    _gla_heads(qg, kg, kd, jnp.exp2(p), jnp.exp2(g_tot), v, row >= col, st_scr, store_o)

    pc = proj("a_c") * proj("a_x")
    t = lax.broadcasted_iota(jnp.int32, (tm, 1), 0) % row_w
    prev = jnp.where(t == 0, 0.0, pltpu.roll(pc, 1, axis=0))
    nxt = jnp.where(t == row_w - 1, 0.0, pltpu.roll(pc, tm - 1, axis=0))
    conv = cw_ref[0:1, :] * prev + cw_ref[1:2, :] * pc + cw_ref[2:3, :] * nxt
    ya_ref[...] = (proj("a_b") * conv * _silu(proj("gate_a"))).astype(BF16)

    @pl.when(i == pl.num_programs(1) - 1)
    def _():
        sfin_ref[...] = st_scr[...]


def _even_in(x, sh, sc, w_main, w_r, layer, conv_w, wg3, bg, s0_f, row_w):
    bn, L, d = x.shape
    tm = min(GLA_CHUNK, L)
    assert L % tm == 0 and tm % row_w == 0
    tile = lambda n: pl.BlockSpec((None, tm, n), lambda b, i: (b, i, 0))
    vec = pl.BlockSpec((None, 1, d), lambda b, i: (b, 0, 0))
    state = pl.BlockSpec((None, GLA_HEADS, GLA_DV, GLA_DK), lambda b, i: (b, 0, 0, 0))
    return pl.pallas_call(
        functools.partial(_even_in_kernel, row_w=row_w),
        grid=(bn, L // tm),
        in_specs=[tile(d), vec, vec, _layer_resident(w_main.shape, layer),
                  _layer_resident(w_r.shape, layer),
                  _resident(conv_w.shape), _resident(wg3.shape), _resident(bg.shape), state],
        out_specs=[tile(GLA_VAL_W), tile(2 * GLA_KEY_W),
                   pl.BlockSpec((None, None, COND_ROWS, GLA_KEY_W), lambda b, i: (b, i, 0, 0)),
                   tile(GLA_VAL_W), tile(CONV_W), tile(GLA_VAL_W), state],
        out_shape=[jax.ShapeDtypeStruct((bn, L, GLA_VAL_W), BF16),
                   jax.ShapeDtypeStruct((bn, L, 2 * GLA_KEY_W), BF16),
                   jax.ShapeDtypeStruct((bn, L // tm, COND_ROWS, GLA_KEY_W), F32),
                   jax.ShapeDtypeStruct((bn, L, GLA_VAL_W), BF16),
                   jax.ShapeDtypeStruct((bn, L, CONV_W), BF16),
                   jax.ShapeDtypeStruct((bn, L, GLA_VAL_W), BF16),
                   jax.ShapeDtypeStruct((bn, GLA_HEADS, GLA_DV, GLA_DK), F32)],
        scratch_shapes=[pltpu.VMEM((GLA_HEADS, GLA_DV, GLA_DK), F32)],
        compiler_params=_params("parallel", "arbitrary"),
        name="even_in",
    )(x, sh, sc, w_main, w_r, conv_w, wg3, bg, s0_f)


def _gla_kernel(fac_ref, scl_ref, v_ref, s0_ref, o_ref, sfin_ref, st_scr, *, chunk):
    c = pl.program_id(1)
    n_sub = v_ref.shape[0] // chunk

    @pl.when(c == 0)
    def _():
        st_scr[...] = s0_ref[...]

    row = lax.broadcasted_iota(jnp.int32, (chunk, chunk), 0)
    col = lax.broadcasted_iota(jnp.int32, (chunk, chunk), 1)
    tri = row <= col

    for s in reversed(range(n_sub)):
        rows = pl.ds(s * chunk, chunk)
        qg = fac_ref[rows, 0:GLA_KEY_W]
        kg = fac_ref[rows, GLA_KEY_W:]
        kd = kg * scl_ref[s, 2:3, :].astype(BF16)

        def store_o(hs, o, rows=rows):
            o_ref[rows, hs] = o

        _gla_heads(qg, kg, kd, scl_ref[s, 0:1, :], scl_ref[s, 1:2, :], v_ref[rows, :], tri,
                   st_scr, store_o)

    @pl.when(c == pl.num_programs(1) - 1)
    def _():
        sfin_ref[...] = st_scr[...]


def _gla(v, fac, scl, s0):
    bn, L, _ = v.shape
    chunk = min(GLA_CHUNK, L)
    blk = min(GLA_BLOCK_CHUNKS * chunk, L)
    nc = L // blk
    assert L % blk == 0 and blk % chunk == 0
    state = pl.BlockSpec((None, GLA_HEADS, GLA_DV, GLA_DK), lambda b, c: (b, 0, 0, 0))
    return pl.pallas_call(
        functools.partial(_gla_kernel, chunk=chunk),
        grid=(bn, nc),
        in_specs=[pl.BlockSpec((None, blk, 2 * GLA_KEY_W), lambda b, c: (b, nc - 1 - c, 0)),
                  pl.BlockSpec((None, blk // chunk, COND_ROWS, GLA_KEY_W),
                               lambda b, c: (b, nc - 1 - c, 0, 0)),
                  pl.BlockSpec((None, blk, GLA_VAL_W), lambda b, c: (b, nc - 1 - c, 0)),
                  state],
        out_specs=[pl.BlockSpec((None, blk, GLA_VAL_W), lambda b, c: (b, nc - 1 - c, 0)), state],
        out_shape=[jax.ShapeDtypeStruct((bn, L, GLA_VAL_W), BF16),
                   jax.ShapeDtypeStruct((bn, GLA_HEADS, GLA_DV, GLA_DK), F32)],
        scratch_shapes=[pltpu.VMEM((GLA_HEADS, GLA_DV, GLA_DK), F32)],
        compiler_params=_params("parallel", "arbitrary"),
        name="gla",
    )(fac, scl, v, s0)


def _residual_ln_rows(x_ref, out_ref, row0, y, gts, g, b):
    for r in range(0, y.shape[0], LN_ROWS):
        rows = pl.ds(row0 + r, LN_ROWS)
        w = x_ref[rows, :] + gts * y[r:r + LN_ROWS, :]
        mu = jnp.mean(w, axis=-1, keepdims=True)
        xc = w - mu
        var = jnp.mean(xc * xc, axis=-1, keepdims=True)
        out_ref[rows, :] = xc * lax.rsqrt(var + LN_EPS / ALPHA ** 2) * g + b


def _gla_branch_rows(of_ref, ob_ref, gb_ref, nw, row0, yb_ref):
    for r in range(0, SUB_TILE, NORM_ROWS):
        rows = pl.ds(row0 + r, NORM_ROWS)
        o = of_ref[rows, :].astype(F32) + ob_ref[rows, :].astype(F32)
        heads = []
        for h in range(GLA_HEADS):
            oh = o[:, h * GLA_DV:(h + 1) * GLA_DV]
            ms = jnp.mean(oh * oh, axis=-1, keepdims=True)
            heads.append(oh * lax.rsqrt(ms + LN_EPS) * nw)
        yb = jnp.concatenate(heads, axis=-1) * gb_ref[rows, :].astype(F32)
        yb_ref[r:r + NORM_ROWS, :] = yb.astype(BF16)


def _even_out_kernel(of_ref, ob_ref, gb_ref, ya_ref, nw_ref, w_ref, x_ref, gt_ref, lg_ref, lb_ref,
                     out_ref, yb_scr):
    n_sub = x_ref.shape[0] // SUB_TILE
    sub = lambda s: pl.ds(s * SUB_TILE, SUB_TILE)
    gts = gt_ref[...] * (1.0 / ALPHA)
    nw, lg, lb = nw_ref[...], lg_ref[...], lb_ref[...]
    _gla_branch_rows(of_ref, ob_ref, gb_ref, nw, 0, yb_scr.at[0])
    y = None
    for s in range(n_sub + 1):
        y_prev = y
        if s < n_sub:
            y = jnp.dot(yb_scr[s % 2], w_ref[0:GLA_VAL_W, :], preferred_element_type=F32)
            y += jnp.dot(ya_ref[sub(s), :], w_ref[GLA_VAL_W:, :], preferred_element_type=F32)
        if s + 1 < n_sub:
            _gla_branch_rows(of_ref, ob_ref, gb_ref, nw, (s + 1) * SUB_TILE, yb_scr.at[(s + 1) % 2])
        if s > 0:
            _residual_ln_rows(x_ref, out_ref, (s - 1) * SUB_TILE, y_prev, gts, lg, lb)


def _even_out(o_f, o_b, gb, ya, norm_w, w_out, layer, x, gt, ln_g, ln_b):
    bn, L, d = x.shape
    tm = min(EVEN_OUT_TILE, L)
    assert L % tm == 0 and tm % SUB_TILE == 0
    tile = lambda n: pl.BlockSpec((None, tm, n), lambda b, i: (b, i, 0))
    return pl.pallas_call(
        _even_out_kernel,
        grid=(bn, L // tm),
        in_specs=[tile(GLA_VAL_W), tile(GLA_VAL_W), tile(GLA_VAL_W), tile(CONV_W),
                  _resident(norm_w.shape),
                  _layer_resident(w_out.shape, layer), tile(d),
                  pl.BlockSpec((None, 1, d), lambda b, i: (b, 0, 0)),
                  _resident(ln_g.shape), _resident(ln_b.shape)],
        out_specs=tile(d),
        out_shape=jax.ShapeDtypeStruct((bn, L, d), F32),
        scratch_shapes=[pltpu.VMEM((2, SUB_TILE, GLA_VAL_W), BF16)],
        compiler_params=_params("parallel", "parallel"),
        name="even_out",
    )(o_f, o_b, gb, ya, norm_w, w_out, x, gt, ln_g, ln_b)


def _out_ln_kernel(m_ref, w_ref, x_ref, gt_ref, lg_ref, lb_ref, out_ref):
    n_sub = x_ref.shape[0] // SUB_TILE
    sub = lambda s: pl.ds(s * SUB_TILE, SUB_TILE)
    gts = gt_ref[...] * (1.0 / ALPHA)
    lg, lb = lg_ref[...], lb_ref[...]
    y = None
    for s in range(n_sub + 1):
        y_prev = y
        if s < n_sub:
            y = jnp.dot(m_ref[sub(s), :], w_ref[...], preferred_element_type=F32)
        if s > 0:
            _residual_ln_rows(x_ref, out_ref, (s - 1) * SUB_TILE, y_prev, gts, lg, lb)


def _out_ln(m, w_out, layer, x, gt, ln_g, ln_b):
    bn, L, d = x.shape
    tm = min(OUT_LN_TILE, L)
    assert L % tm == 0 and tm % SUB_TILE == 0
    tile = lambda n: pl.BlockSpec((None, tm, n), lambda b, i: (b, i, 0))
    return pl.pallas_call(
        _out_ln_kernel,
        grid=(bn, L // tm),
        in_specs=[tile(m.shape[-1]), _layer_resident(w_out.shape, layer), tile(d),
                  pl.BlockSpec((None, 1, d), lambda b, i: (b, 0, 0)),
                  _resident(ln_g.shape), _resident(ln_b.shape)],
        out_specs=tile(d),
        out_shape=jax.ShapeDtypeStruct((bn, L, d), F32),
        compiler_params=_params("parallel", "parallel"),
        name="out_ln",
    )(m, w_out, x, gt, ln_g, ln_b)


def _odd_mix_kernel(xp_ref, xc_ref, xn_ref, sh_ref, sc_ref, win_ref, wpool_ref, ps_ref, m_ref,
                    *, row_stride, n_rows):
    i = pl.program_id(1)
    tm = xc_ref.shape[0]
    halo = POOL_HALO_ROWS * row_stride
    xc = xc_ref[...]
    scale1 = 1.0 + sc_ref[...]
    u = (xc * scale1 + sh_ref[...]).astype(BF16)
    sgate = _silu(jnp.dot(u, win_ref[:, POOL_W:], preferred_element_type=F32)).astype(BF16)

    before = jnp.where(i > 0, xp_ref[tm - halo:, :], 0.0)
    after = jnp.where(i < pl.num_programs(1) - 1, xn_ref[:halo, :], 0.0)
    xe = jnp.concatenate([before, xc, after], axis=0)
    grid_row = (i * tm + lax.broadcasted_iota(jnp.int32, (tm, 1), 0)) // row_stride

    def window_diff(s, w):
        half = w // 2
        s = s[:s.shape[0] - half * row_stride, :] + s[half * row_stride:, :]
        start = halo - half * row_stride
        cnt = jnp.minimum(grid_row + half, n_rows) - jnp.maximum(grid_row - half, 0)
        mean = s[start:start + tm, :] / cnt.astype(F32)
        return s, ((mean - xc) * scale1).astype(BF16)

    s, diff = window_diff(xe, POOL_WINDOWS[0])
    for g in range(len(POOL_WINDOWS)):
        cur = diff
        if g + 1 < len(POOL_WINDOWS):
            s, diff = window_diff(s, POOL_WINDOWS[g + 1])
        cols = slice(g * POOL_GROUP, (g + 1) * POOL_GROUP)
        z = jnp.dot(cur, win_ref[:, cols], preferred_element_type=F32)
        z = jnp.dot(z.astype(BF16), wpool_ref[g], preferred_element_type=F32)
        m_ref[:, cols] = (z * ps_ref[:, cols] * sgate[:, cols].astype(F32)).astype(BF16)


def _odd_mix(x, sh, sc, w_in, w_pool, layer, pool_scale, row_stride):
    bn, L, d = x.shape
    n_rows = L // row_stride
    tm = min(POOL_TILE, L)
    nt = L // tm
    assert L % tm == 0 and tm % row_stride == 0 and POOL_HALO_ROWS * row_stride <= tm
    vec = pl.BlockSpec((None, 1, d), lambda b, i: (b, 0, 0))
    return pl.pallas_call(
        functools.partial(_odd_mix_kernel, row_stride=row_stride, n_rows=n_rows),
        grid=(bn, nt),
        in_specs=[pl.BlockSpec((None, tm, d), lambda b, i: (b, jnp.maximum(i - 1, 0), 0)),
                  pl.BlockSpec((None, tm, d), lambda b, i: (b, i, 0)),
                  pl.BlockSpec((None, tm, d), lambda b, i: (b, jnp.minimum(i + 1, nt - 1), 0)),
                  vec, vec, _layer_resident(w_in.shape, layer), _layer_resident(w_pool.shape, layer),
                  _resident(pool_scale.shape)],
        out_specs=pl.BlockSpec((None, tm, POOL_W), lambda b, i: (b, i, 0)),
        out_shape=jax.ShapeDtypeStruct((bn, L, POOL_W), BF16),
        compiler_params=_params("parallel", "parallel"),
        name="odd_mix",
    )(x, x, x, sh, sc, w_in, w_pool, pool_scale)


def _hi_lo(w):
    hi = w.astype(BF16)
    return hi, (w - hi.astype(F32)).astype(BF16)


def _even_w_prep_kernel(wt_ref, rank_ref, main_ref, r_ref):
    main_ref[...] = wt_ref[0].T.astype(BF16)
    rank = rank_ref[0].T[:, LANES - RANK_COLS:].astype(BF16)
    pad = jnp.zeros((rank.shape[0], LANES - 3 * RANK_COLS), BF16)
    r_ref[...] = jnp.concatenate([rank, rank, rank, pad], axis=1)


def _even_w_prep(w_in_e):
    ne, d, n = w_in_e.shape
    tn = PREP_COLS
    assert STATE_W % tn == 0 and (n - RANK_COLS) % tn == 0
    wt = jnp.transpose(w_in_e, (0, 2, 1))

    def src_row(j, b):
        row = b * tn + jnp.where(b * tn >= STATE_W, RANK_COLS, 0)
        return (j, pl.multiple_of(row, RANK_COLS), 0)

    return pl.pallas_call(
        _even_w_prep_kernel,
        grid=(ne, (n - RANK_COLS) // tn),
        in_specs=[pl.BlockSpec((pl.Element(1), pl.Element(tn), pl.Element(d)), src_row),
                  pl.BlockSpec((pl.Element(1), pl.Element(LANES), pl.Element(d)),
                               lambda j, b: (j, STATE_W + RANK_COLS - LANES, 0))],
        out_specs=[pl.BlockSpec((None, d, tn), lambda j, b: (j, 0, b)),
                   pl.BlockSpec((None, d, LANES), lambda j, b: (j, 0, 0))],
        out_shape=[jax.ShapeDtypeStruct((ne, d, n - RANK_COLS), BF16),
                   jax.ShapeDtypeStruct((ne, d, LANES), BF16)],
        compiler_params=_params("parallel", "arbitrary"),
        name="even_w_prep",
    )(wt, wt)


def _gate_weights(w_gf, b_gf, w_gb, b_gb):
    zr = jnp.zeros((GLA_GATE_RANK, GLA_KEY_W), BF16)
    tail = jnp.zeros((LANES - 3 * RANK_COLS, GLA_KEY_W), BF16)
    f_hi, f_lo = _hi_lo(w_gf * LOG2_E)
    b_hi, b_lo = _hi_lo(w_gb * LOG2_E)
    wg3 = jnp.stack([jnp.concatenate([f_hi, zr, f_hi, zr, f_lo, zr, tail], axis=0),
                     jnp.concatenate([zr, b_hi, zr, b_hi, zr, b_lo, tail], axis=0)])
    bg = jnp.stack([b_gf, b_gb])[:, None, :] * LOG2_E
    return wg3, bg


def kernel(x, c, ctx, c_ctx, w_ada, b_ada, ln_g, ln_b, w_in_e, w_gate_f, b_gate_f, w_gate_b,
           b_gate_b, gla_norm_w, conv_w, w_out_e, w_in_o, w_pool, pool_scale, w_out_o):
    bn, L, d = x.shape
    lc = ctx.shape[1]
    cond = jnp.concatenate([c, c_ctx[None, :], jnp.zeros((COND_ROWS - bn - 1, d), F32)], axis=0)
    mod = _modulation(cond, w_ada, b_ada)
    s_zero = jnp.zeros((bn, GLA_HEADS, GLA_DV, GLA_DK), F32)
    w_main, w_r = _even_w_prep(w_in_e)
    w_out_e, w_in_o, w_out_o, w_pool = (
        w.astype(BF16) for w in (w_out_e, w_in_o, w_out_o, w_pool))
    ctx_s = ctx
    for i in range(DEPTH):
        j = i // 2
        ctx_needed = any(l % 2 == 0 for l in range(i + 1, DEPTH))
        sh, sc, gt = (mod[i, :bn, n * d:(n + 1) * d][:, None, :] for n in range(3))
        sh_c, sc_c, gt_c = (jnp.broadcast_to(mod[i, bn, n * d:(n + 1) * d], (bn, 1, d))
                            for n in range(3))
        lg, lb = ln_g[i][None, :], ln_b[i][None, :]
        if i % 2 == 0:
            wg3, bg = _gate_weights(w_gate_f[j], b_gate_f[j], w_gate_b[j], b_gate_b[j])
            norm_w = gla_norm_w[j][None, :]
            v_c, fac_c, scl_c, gb_c, ya_c, of_c, sf_c = _even_in(
                ctx_s, sh_c, sc_c, w_main, w_r, j, conv_w[j], wg3, bg, s_zero, lc)
            ob_c, sb_c = _gla(v_c, fac_c, scl_c, s_zero)
            if ctx_needed:
                ctx_s = _even_out(of_c, ob_c, gb_c, ya_c, norm_w, w_out_e, j, ctx_s, gt_c, lg, lb)
            v, fac, scl, gb, ya, o_f, _ = _even_in(x, sh, sc, w_main, w_r, j, conv_w[j], wg3, bg,
                                                   sf_c, GRID_W)
            o_b, _ = _gla(v, fac, scl, sb_c)
            x = _even_out(o_f, o_b, gb, ya, norm_w, w_out_e, j, x, gt, lg, lb)
        else:
            ps = pool_scale[j][None, :]
            if ctx_needed:
                m_c = _odd_mix(ctx_s, sh_c, sc_c, w_in_o, w_pool, j, ps, 1)
                ctx_s = _out_ln(m_c, w_out_o, j, ctx_s, gt_c, lg, lb)
            m = _odd_mix(x, sh, sc, w_in_o, w_pool, j, ps, GRID_W)
            x = _out_ln(m, w_out_o, j, x, gt, lg, lb)
    return x
```
